```python
import jax, jax.numpy as jnp
from jax import lax
import numpy as np

D_MODEL = 1024
BATCH = 4
SEQ = 4096
DEPTH = 2

HEAD_DIM = 64
N_HEADS = D_MODEL // HEAD_DIM
N_SB_HEADS = N_HEADS // 2
N_CA_HEADS = N_HEADS - N_SB_HEADS
D_SB = N_SB_HEADS * HEAD_DIM
D_CA = N_CA_HEADS * HEAD_DIM
D_IN = 3 * D_SB + 3 * D_CA
D_FF = 4 * D_MODEL
CHUNK = 64
LEFT_CHUNKS = 8
BAND = (LEFT_CHUNKS + 1) * CHUNK
REL_CLIP = 128
N_REL = 2 * REL_CLIP + 1
Q_BLOCK = 128
EPS = 1e-6
NEG_INF = -1e30

kernel_name = "hybrid_stickbreak_chunkrel_adaln_encoder"


def rmsnorm(x, g):
    xf = x.astype(jnp.float32)
    y = xf * lax.rsqrt(jnp.mean(xf * xf, axis=-1, keepdims=True) + EPS)
    return (y * g.astype(jnp.float32)).astype(x.dtype)


def stick_breaking_attention(q, k, v):
    B, S, H, d = q.shape
    scale = d ** -0.5
    outs = []
    for start in range(0, S, Q_BLOCK):
        end = start + Q_BLOCK
        qb = q[:, start:end]
        kb = k[:, :end]
        vb = v[:, :end]
        z = jnp.einsum('bqhd,bkhd->bhqk', qb, kb).astype(jnp.float32) * scale
        t_idx = start + jnp.arange(Q_BLOCK)[:, None]
        s_idx = jnp.arange(end)[None, :]
        strict = s_idx < t_idx
        log_beta = jax.nn.log_sigmoid(z)
        log_1m_beta = jnp.where(strict, jax.nn.log_sigmoid(-z), 0.0)
        suffix = lax.cumsum(log_1m_beta, axis=3, reverse=True) - log_1m_beta
        w = jnp.where(strict, jnp.exp(log_beta + suffix), 0.0)
        outs.append(jnp.einsum('bhqk,bkhd->bqhd', w.astype(v.dtype), vb))
    return jnp.concatenate(outs, axis=1)


def chunked_relpos_attention(q, k, v, rel_bias):
    B, S, H, d = q.shape
    nc = S // CHUNK
    pad = LEFT_CHUNKS * CHUNK
    kp = jnp.pad(k, ((0, 0), (pad, 0), (0, 0), (0, 0))).reshape(B, nc + LEFT_CHUNKS, CHUNK, H, d)
    vp = jnp.pad(v, ((0, 0), (pad, 0), (0, 0), (0, 0))).reshape(B, nc + LEFT_CHUNKS, CHUNK, H, d)
    k_band = jnp.concatenate([kp[:, i:i + nc] for i in range(LEFT_CHUNKS + 1)], axis=2)
    v_band = jnp.concatenate([vp[:, i:i + nc] for i in range(LEFT_CHUNKS + 1)], axis=2)
    qc = q.reshape(B, nc, CHUNK, H, d)
    s = jnp.einsum('bnqhd,bnkhd->bnhqk', qc, k_band).astype(jnp.float32) * (d ** -0.5)
    qi = jnp.arange(CHUNK)[:, None]
    kj = jnp.arange(BAND)[None, :]
    rel = qi + pad - kj
    rel_idx = jnp.clip(rel, -REL_CLIP, REL_CLIP) + REL_CLIP
    bias = rel_bias[:, rel_idx].astype(jnp.float32)
    s = s + bias[None, None]
    key_pos = jnp.arange(nc)[:, None] * CHUNK + kj - pad
    valid = key_pos >= 0
    s = jnp.where(valid[None, :, None, None, :], s, NEG_INF)
    p = jax.nn.softmax(s, axis=-1)
    out = jnp.einsum('bnhqk,bnkhd->bnqhd', p.astype(v.dtype), v_band)
    return out.reshape(B, S, H, d)


def setup_inputs(seed: int = 0) -> dict:
    key = jax.random.key(seed)
    ks = jax.random.split(key, 16)
    f32 = jnp.float32
    x = jax.random.normal(ks[0], (BATCH, SEQ, D_MODEL), f32)
    c = jax.random.normal(ks[1], (BATCH, D_MODEL), f32)
    g_norm1 = 1.0 + 0.01 * jax.random.normal(ks[2], (DEPTH, D_MODEL), f32)
    w_in = jax.random.normal(ks[3], (DEPTH, D_MODEL, D_IN), f32) * D_MODEL ** -0.5
    g_q = 1.0 + 0.01 * jax.random.normal(ks[4], (DEPTH, HEAD_DIM), f32)
    g_k = 1.0 + 0.01 * jax.random.normal(ks[5], (DEPTH, HEAD_DIM), f32)
    rel_bias = 0.1 * jax.random.normal(ks[6], (DEPTH, N_CA_HEADS, N_REL), f32)
    w_o = jax.random.normal(ks[7], (DEPTH, D_MODEL, D_MODEL), f32) * D_MODEL ** -0.5
    g_norm2 = 1.0 + 0.01 * jax.random.normal(ks[8], (DEPTH, D_MODEL), f32)
    w1 = jax.random.normal(ks[9], (DEPTH, D_MODEL, D_FF), f32) * D_MODEL ** -0.5
    w2 = jax.random.normal(ks[10], (DEPTH, D_FF, D_MODEL), f32) * D_FF ** -0.5
    w_ada = jax.random.normal(ks[11], (DEPTH, D_MODEL, 6 * D_MODEL), f32) * (0.5 * D_MODEL ** -0.5)
    b_ada = 0.01 * jax.random.normal(ks[12], (DEPTH, 6 * D_MODEL), f32)
    return {"x": x, "c": c, "g_norm1": g_norm1, "w_in": w_in, "g_q": g_q, "g_k": g_k,
            "rel_bias": rel_bias, "w_o": w_o, "g_norm2": g_norm2, "w1": w1, "w2": w2,
            "w_ada": w_ada, "b_ada": b_ada}


def reference(x, c, g_norm1, w_in, g_q, g_k, rel_bias, w_o, g_norm2, w1, w2, w_ada, b_ada):
    B, S, D = x.shape
    split_pts = [D_SB, 2 * D_SB, 3 * D_SB, 3 * D_SB + D_CA, 3 * D_SB + 2 * D_CA]
    c_act = jax.nn.silu(c)
    for l in range(DEPTH):
        mod = c_act @ w_ada[l] + b_ada[l]
        sh1, sc1, gt1, sh2, sc2, gt2 = [m[:, None, :] for m in jnp.split(mod, 6, axis=-1)]
        h = rmsnorm(x, g_norm1[l]) * (1.0 + sc1) + sh1
        proj = h @ w_in[l]
        q_sb, k_sb, v_sb, q_ca, k_ca, v_ca = jnp.split(proj, split_pts, axis=-1)
        hs = lambda t, n: t.reshape(B, S, n, HEAD_DIM)
        o_sb = stick_breaking_attention(hs(q_sb, N_SB_HEADS), hs(k_sb, N_SB_HEADS), hs(v_sb, N_SB_HEADS))
        q_ca = rmsnorm(hs(q_ca, N_CA_HEADS), g_q[l])
        k_ca = rmsnorm(hs(k_ca, N_CA_HEADS), g_k[l])
        o_ca = chunked_relpos_attention(q_ca, k_ca, hs(v_ca, N_CA_HEADS), rel_bias[l])
        mixed = jnp.concatenate([o_sb.reshape(B, S, D_SB), o_ca.reshape(B, S, D_CA)], axis=-1)
        x = x + gt1 * (mixed @ w_o[l])
        h = rmsnorm(x, g_norm2[l]) * (1.0 + sc2) + sh2
        x = x + gt2 * (jnp.square(jax.nn.relu(h @ w1[l])) @ w2[l])
    return x
```

```python
import functools

import jax
import jax.numpy as jnp
from jax import lax
from jax.experimental import pallas as pl
from jax.experimental.pallas import tpu as pltpu

F32 = jnp.float32
BF16 = jnp.bfloat16

HEAD_DIM = 64
LANES = 128
HEADS_PER_GROUP = 8
PAIRS = HEADS_PER_GROUP // 2
D_GROUP = HEADS_PER_GROUP * HEAD_DIM
CHUNK = 64
LEFT_CHUNKS = 8
BAND = (LEFT_CHUNKS + 1) * CHUNK
REL_CLIP = 128
EPS = 1e-6
NEG_INF = -1e30
QK_SCALE = HEAD_DIM ** -0.5

ROW_TILE = 512
SB_TQ = 256
SB_TK = 128
CA_TQ = 256
CA_WIN = CA_TQ + LEFT_CHUNKS * CHUNK
SB_DEAD = -106.0
VMEM_LIMIT = 56 * 1024 * 1024


def _cparams(n_axes):
    return pltpu.CompilerParams(dimension_semantics=("arbitrary",) * n_axes,
                                vmem_limit_bytes=VMEM_LIMIT)


def _resident(shape, index_map):
    return pl.BlockSpec(shape, index_map, pipeline_mode=pl.Buffered(1))


def _ada_kernel(c_ref, w_ref, b_ref, o_ref):
    ca = jax.nn.silu(c_ref[...]).astype(BF16)
    o_ref[...] = jnp.dot(ca, w_ref[...].astype(BF16), preferred_element_type=F32) + b_ref[...]


def _ada_modulation(c, w_ada, b_ada):
    depth, d, n = w_ada.shape
    b = c.shape[0]
    rows = 8
    c_pad = jnp.pad(c, ((0, rows - b), (0, 0)))
    tn = d
    out = pl.pallas_call(
        _ada_kernel,
        grid=(depth, n // tn),
        in_specs=[pl.BlockSpec((rows, d), lambda l, j: (0, 0)),
                  pl.BlockSpec((None, d, tn), lambda l, j: (l, 0, j)),
                  pl.BlockSpec((None, 1, tn), lambda l, j: (l, 0, j))],
        out_specs=pl.BlockSpec((None, rows, tn), lambda l, j: (l, 0, j)),
        out_shape=jax.ShapeDtypeStruct((depth, rows, n), F32),
        compiler_params=_cparams(2),
        name="ada_modulation",
    )(c_pad, w_ada, b_ada.reshape(depth, 1, n))
    return out[:, :b].reshape(depth, b, 6, d)


def _modulated_norm(x, g, shift, scale):
    ms = jnp.mean(x * x, axis=-1, keepdims=True)
    return (x * lax.rsqrt(ms + EPS) * g) * (1.0 + scale) + shift


def _inproj_kernel(x_ref, mod_ref, g1_ref, w_ref, gq_ref, gk_ref, gmat_ref, o_ref):
    h = _modulated_norm(x_ref[...], g1_ref[...], mod_ref[0:1, :], mod_ref[1:2, :]).astype(BF16)
    for c in range(6):
        cols = slice(c * D_GROUP, (c + 1) * D_GROUP)
        y = jnp.dot(h, w_ref[:, cols], preferred_element_type=F32)
        if c == 0:
            y = y * QK_SCALE
        elif c in (3, 4):
            msq = jnp.dot((y * y).astype(BF16), gmat_ref[...], preferred_element_type=F32)
            y = y * lax.rsqrt(msq + EPS) * (gq_ref[...] if c == 3 else gk_ref[...])
        o_ref[:, cols] = y.astype(BF16)


def _inproj(x, mod_l, g1, w_in_bf, gq_t, gk_t, gmat):
    b, s, d = x.shape
    n = w_in_bf.shape[1]
    tm = min(ROW_TILE, s)
    return pl.pallas_call(
        _inproj_kernel,
        grid=(b, s // tm),
        in_specs=[pl.BlockSpec((None, tm, d), lambda i, j: (i, j, 0)),
                  pl.BlockSpec((None, 6, d), lambda i, j: (i, 0, 0)),
                  _resident((1, d), lambda i, j: (0, 0)),
                  _resident((d, n), lambda i, j: (0, 0)),
                  _resident((1, D_GROUP), lambda i, j: (0, 0)),
                  _resident((1, D_GROUP), lambda i, j: (0, 0)),
                  _resident((D_GROUP, D_GROUP), lambda i, j: (0, 0))],
        out_specs=pl.BlockSpec((None, tm, n), lambda i, j: (i, j, 0)),
        out_shape=jax.ShapeDtypeStruct((b, s, n), BF16),
        compiler_params=_cparams(2),
        name="norm1_inproj",
    )(x, mod_l, g1, w_in_bf, gq_t, gk_t, gmat)


def _head_masked(q):
    lane = lax.broadcasted_iota(jnp.int32, q.shape, 1)
    zero = jnp.zeros_like(q)
    return jnp.where(lane < HEAD_DIM, q, zero), jnp.where(lane >= HEAD_DIM, q, zero)


def _scores_t(k_blk, q_masked):
    return lax.dot_general(k_blk, q_masked, (((1,), (1,)), ((), ())), preferred_element_type=F32)


def _fill_vt(v_ref, vt_ref, blk):
    for i in range(vt_ref.shape[0]):
        vt_ref[i] = v_ref[i * blk:(i + 1) * blk, :].astype(F32).T.astype(BF16)


def _merge_heads_t(out_a, out_b):
    row = lax.broadcasted_iota(jnp.int32, out_a.shape, 0)
    return jnp.where(row < HEAD_DIM, out_a, out_b).T


def _sb_kernel(q_ref, k_ref, v_ref, o_ref, vt_ref, acc_ref):
    tq = q_ref.shape[0]
    tk = vt_ref.shape[2]
    qi = pl.program_id(2)

    @pl.when(qi == 0)
    def _():
        _fill_vt(v_ref, vt_ref, tk)

    q_heads = _head_masked(q_ref[...])
    acc_ref[...] = jnp.zeros_like(acc_ref)
    tri = (lax.broadcasted_iota(jnp.int32, (tk, tk), 1) >
           lax.broadcasted_iota(jnp.int32, (tk, tk), 0)).astype(BF16)
    row_minus_lane = (lax.broadcasted_iota(jnp.int32, (tk, tq), 0) -
                      lax.broadcasted_iota(jnp.int32, (tk, tq), 1))
    q_start = qi * tq

    def block(kb, carries):
        k_start = pl.multiple_of(kb * tk, tk)
        k_blk = k_ref[pl.ds(k_start, tk), :]
        vt_blk = vt_ref[kb]
        strict = row_minus_lane < (q_start - k_start)
        new = []
        for h in range(2):
            z = _scores_t(k_blk, q_heads[h])
            log1p_e = jnp.log(1.0 + jnp.exp(-jnp.abs(z)))
            log_beta = jnp.minimum(z, 0.0) - log1p_e
            log_rest = jnp.where(strict, log_beta - z, 0.0)
            hi = log_rest.astype(BF16)
            lo = (log_rest - hi.astype(F32)).astype(BF16)
            suffix = (jnp.dot(tri, hi, preferred_element_type=F32) +
                      jnp.dot(tri, lo, preferred_element_type=F32))
            w = jnp.where(strict, jnp.exp(log_beta + suffix + carries[h]), 0.0).astype(BF16)
            acc_ref[h] += jnp.dot(vt_blk, w, preferred_element_type=F32)
            new.append(carries[h] + suffix[0:1, :] + log_rest[0:1, :])
        return tuple(new)

    def cond(state):
        kb, ca, cb = state
        alive = jnp.max(jnp.maximum(ca, cb)) > SB_DEAD
        return jnp.logical_and(kb >= 0, alive)

    def body(state):
        kb, ca, cb = state
        ca, cb = block(kb, (ca, cb))
        return kb - 1, ca, cb

    zero = jnp.zeros((1, tq), F32)
    lax.while_loop(cond, body, (q_start // tk + tq // tk - 1, zero, zero))
    o_ref[...] = _merge_heads_t(acc_ref[0], acc_ref[1]).astype(o_ref.dtype)


def _sb_attention(proj):
    b, s, _ = proj.shape
    tq = min(SB_TQ, s)
    tk = min(SB_TK, tq)
    return pl.pallas_call(
        _sb_kernel,
        grid=(b, PAIRS, s // tq),
        in_specs=[pl.BlockSpec((None, tq, LANES), lambda i, p, j: (i, j, p)),
                  pl.BlockSpec((None, s, LANES), lambda i, p, j: (i, 0, PAIRS + p)),
                  pl.BlockSpec((None, s, LANES), lambda i, p, j: (i, 0, 2 * PAIRS + p))],
        out_specs=pl.BlockSpec((None, tq, LANES), lambda i, p, j: (i, j, p)),
        out_shape=jax.ShapeDtypeStruct((b, s, D_GROUP), BF16),
        scratch_shapes=[pltpu.VMEM((s // tk, LANES, tk), BF16),
                        pltpu.VMEM((2, LANES, tq), F32)],
        compiler_params=_cparams(3),
        name="stickbreak_attn",
    )(proj, proj, proj)


def _ca_kernel(q_ref, k_ref, v_ref, bias_ref, o_ref, vt_ref):
    tq = q_ref.shape[0]
    n_win = CA_WIN // tq
    qi = pl.program_id(2)

    @pl.when(qi == 0)
    def _():
        _fill_vt(v_ref, vt_ref, tq)

    q_heads = _head_masked(q_ref[...])
    blks = [jnp.maximum(qi - (n_win - 1) + i, 0) for i in range(n_win)]
    k_blks = [k_ref[pl.ds(pl.multiple_of(bi * tq, tq), tq), :] for bi in blks]
    outs = []
    for h in range(2):
        s_blks = []
        for i in range(n_win):
            sc = _scores_t(k_blks[i], q_heads[h]) + bias_ref[h, i * tq:(i + 1) * tq, :]
            s_blks.append(jnp.where(qi + i >= n_win - 1, sc, NEG_INF))
        m = s_blks[0].max(axis=0, keepdims=True)
        for sc in s_blks[1:]:
            m = jnp.maximum(m, sc.max(axis=0, keepdims=True))
        acc = jnp.zeros((LANES, tq), F32)
        denom = jnp.zeros((1, tq), F32)
        for i in range(n_win):
            p = jnp.exp(s_blks[i] - m)
            denom = denom + p.sum(axis=0, keepdims=True)
            acc = acc + jnp.dot(vt_ref[blks[i]], p.astype(BF16), preferred_element_type=F32)
        outs.append(acc * (1.0 / denom))
    o_ref[...] = _merge_heads_t(outs[0], outs[1]).astype(o_ref.dtype)


def _ca_attention(proj, bias_tab):
    b, s, _ = proj.shape
    tq = CA_TQ
    base = 3 * PAIRS
    return pl.pallas_call(
        _ca_kernel,
        grid=(b, PAIRS, s // tq),
        in_specs=[pl.BlockSpec((None, tq, LANES), lambda i, p, j: (i, j, base + p)),
                  pl.BlockSpec((None, s, LANES), lambda i, p, j: (i, 0, base + PAIRS + p)),
                  pl.BlockSpec((None, s, LANES), lambda i, p, j: (i, 0, base + 2 * PAIRS + p)),
                  pl.BlockSpec((2, CA_WIN, tq), lambda i, p, j: (p, 0, 0))],
        out_specs=pl.BlockSpec((None, tq, LANES), lambda i, p, j: (i, j, p)),
        out_shape=jax.ShapeDtypeStruct((b, s, D_GROUP), BF16),
        scratch_shapes=[pltpu.VMEM((s // tq, LANES, tq), BF16)],
        compiler_params=_cparams(3),
        name="chunkrel_attn",
    )(proj, proj, proj, bias_tab)


def _ca_bias_table(rel_bias_l):
    qi = jnp.arange(CHUNK)[None, :]
    kj = jnp.arange(BAND)[:, None]
    rel_idx = jnp.clip(qi + LEFT_CHUNKS * CHUNK - kj, -REL_CLIP, REL_CLIP) + REL_CLIP
    band = rel_bias_l[:, rel_idx].astype(F32)
    tab = jnp.full((rel_bias_l.shape[0], CA_WIN, CA_TQ), NEG_INF, F32)
    for c in range(CA_TQ // CHUNK):
        tab = tab.at[:, c * CHUNK:c * CHUNK + BAND, c * CHUNK:(c + 1) * CHUNK].set(band)
    return tab


def _mlp_kernel(x_ref, osb_ref, oca_ref, mod_ref, g2_ref, wo_ref, w1_ref, w2_ref, o_ref):
    att = (jnp.dot(osb_ref[...], wo_ref[0:D_GROUP, :], preferred_element_type=F32) +
           jnp.dot(oca_ref[...], wo_ref[D_GROUP:, :], preferred_element_type=F32))
    x1 = x_ref[...] + mod_ref[2:3, :] * att
    h = _modulated_norm(x1, g2_ref[...], mod_ref[3:4, :], mod_ref[4:5, :]).astype(BF16)
    d_ff = w1_ref.shape[1]
    fc = 1024
    acc = jnp.zeros(x1.shape, F32)
    for c in range(d_ff // fc):
        u = jnp.maximum(jnp.dot(h, w1_ref[:, c * fc:(c + 1) * fc], preferred_element_type=F32), 0.0)
        acc = acc + jnp.dot((u * u).astype(BF16), w2_ref[c * fc:(c + 1) * fc, :],
                            preferred_element_type=F32)
    o_ref[...] = x1 + mod_ref[5:6, :] * acc


def _outproj_mlp(x, o_sb, o_ca, mod_l, g2, wo_bf, w1_bf, w2_bf):
    b, s, d = x.shape
    d_ff = w1_bf.shape[1]
    tm = min(ROW_TILE, s)
    return pl.pallas_call(
        _mlp_kernel,
        grid=(b, s // tm),
        in_specs=[pl.BlockSpec((None, tm, d), lambda i, j: (i, j, 0)),
                  pl.BlockSpec((None, tm, D_GROUP), lambda i, j: (i, j, 0)),
                  pl.BlockSpec((None, tm, D_GROUP), lambda i, j: (i, j, 0)),
                  pl.BlockSpec((None, 6, d), lambda i, j: (i, 0, 0)),
                  _resident((1, d), lambda i, j: (0, 0)),
                  _resident((d, d), lambda i, j: (0, 0)),
                  _resident((d, d_ff), lambda i, j: (0, 0)),
                  _resident((d_ff, d), lambda i, j: (0, 0))],
        out_specs=pl.BlockSpec((None, tm, d), lambda i, j: (i, j, 0)),
        out_shape=jax.ShapeDtypeStruct((b, s, d), F32),
        compiler_params=_cparams(2),
        name="outproj_mlp",
    )(x, o_sb, o_ca, mod_l, g2, wo_bf, w1_bf, w2_bf)


def kernel(x, c, g_norm1, w_in, g_q, g_k, rel_bias, w_o, g_norm2, w1, w2, w_ada, b_ada):
    depth = w_in.shape[0]
    mod = _ada_modulation(c, w_ada, b_ada)
    lane_head = jnp.arange(D_GROUP) // HEAD_DIM
    gmat = (lane_head[:, None] == lane_head[None, :]).astype(BF16) * (1.0 / HEAD_DIM)
    gmat = gmat.astype(BF16)
    for l in range(depth):
        gq_t = (jnp.tile(g_q[l], HEADS_PER_GROUP) * QK_SCALE)[None, :]
        gk_t = jnp.tile(g_k[l], HEADS_PER_GROUP)[None, :]
        proj = _inproj(x, mod[l], g_norm1[l][None, :], w_in[l].astype(BF16), gq_t, gk_t, gmat)
        o_sb = _sb_attention(proj)
        o_ca = _ca_attention(proj, _ca_bias_table(rel_bias[l]))
        x = _outproj_mlp(x, o_sb, o_ca, mod[l], g_norm2[l][None, :],
                         w_o[l].astype(BF16), w1[l].astype(BF16), w2[l].astype(BF16))
    return x
```

```python
import jax
import jax.numpy as jnp
from jax import lax
from jax.experimental import pallas as pl
from jax.experimental.pallas import tpu as pltpu

F32 = jnp.float32
BF16 = jnp.bfloat16

HEAD_DIM = 64
LANES = 128
HEADS_PER_GROUP = 8
PAIRS = HEADS_PER_GROUP // 2
D_GROUP = HEADS_PER_GROUP * HEAD_DIM
CHUNK = 64
LEFT_CHUNKS = 8
BAND = (LEFT_CHUNKS + 1) * CHUNK
REL_CLIP = 128
EPS = 1e-6
NEG_INF = -1e30
QK_SCALE = HEAD_DIM ** -0.5
LOG2E = 1.4426950408889634

ROW_TILE = 512
SB_TQ = 256
SB_TILES = 2
CA_TQ = 256
CA_TILES = 2
CA_WIN = CA_TQ + LEFT_CHUNKS * CHUNK
SB_DEAD = -106.0
VMEM_LIMIT = 56 * 1024 * 1024


def _cparams(n_axes):
    return pltpu.CompilerParams(dimension_semantics=("arbitrary",) * n_axes,
                                vmem_limit_bytes=VMEM_LIMIT)


def _resident(shape, index_map):
    return pl.BlockSpec(shape, index_map, pipeline_mode=pl.Buffered(1))


def _ada_kernel(c_ref, w_ref, b_ref, o_ref):
    ca = jax.nn.silu(c_ref[...]).astype(BF16)
    o_ref[...] = jnp.dot(ca, w_ref[...].astype(BF16), preferred_element_type=F32) + b_ref[...]


def _ada_modulation(c, w_ada, b_ada):
    depth, d, n = w_ada.shape
    b = c.shape[0]
    rows = 8
    c_pad = jnp.pad(c, ((0, rows - b), (0, 0)))
    tn = d
    out = pl.pallas_call(
        _ada_kernel,
        grid=(depth, n // tn),
        in_specs=[pl.BlockSpec((rows, d), lambda l, j: (0, 0)),
                  pl.BlockSpec((None, d, tn), lambda l, j: (l, 0, j)),
                  pl.BlockSpec((None, 1, tn), lambda l, j: (l, 0, j))],
        out_specs=pl.BlockSpec((None, rows, tn), lambda l, j: (l, 0, j)),
        out_shape=jax.ShapeDtypeStruct((depth, rows, n), F32),
        compiler_params=_cparams(2),
        name="ada_modulation",
    )(c_pad, w_ada, b_ada.reshape(depth, 1, n))
    return out[:, :b].reshape(depth, b, 6, d)


def _modulated_norm(x, g, shift, scale):
    ms = jnp.mean(x * x, axis=-1, keepdims=True)
    return (x * lax.rsqrt(ms + EPS) * g) * (1.0 + scale) + shift


def _inproj_kernel(x_ref, mod_ref, g1_ref, w_ref, gq_ref, gk_ref, gmat_ref, o_ref):
    h = _modulated_norm(x_ref[...], g1_ref[...], mod_ref[0:1, :], mod_ref[1:2, :]).astype(BF16)
    for c in range(6):
        cols = slice(c * D_GROUP, (c + 1) * D_GROUP)
        y = jnp.dot(h, w_ref[:, cols], preferred_element_type=F32)
        if c == 0:
            y = y * QK_SCALE
        elif c in (3, 4):
            msq = jnp.dot((y * y).astype(BF16), gmat_ref[...], preferred_element_type=F32)
            y = y * lax.rsqrt(msq + EPS) * (gq_ref[...] if c == 3 else gk_ref[...])
        o_ref[:, cols] = y.astype(BF16)


def _inproj(x, mod_l, g1, w_in_bf, gq_t, gk_t, gmat):
    b, s, d = x.shape
    n = w_in_bf.shape[1]
    tm = min(ROW_TILE, s)
    return pl.pallas_call(
        _inproj_kernel,
        grid=(b, s // tm),
        in_specs=[pl.BlockSpec((None, tm, d), lambda i, j: (i, j, 0)),
                  pl.BlockSpec((None, 6, d), lambda i, j: (i, 0, 0)),
                  _resident((1, d), lambda i, j: (0, 0)),
                  _resident((d, n), lambda i, j: (0, 0)),
                  _resident((1, D_GROUP), lambda i, j: (0, 0)),
                  _resident((1, D_GROUP), lambda i, j: (0, 0)),
                  _resident((D_GROUP, D_GROUP), lambda i, j: (0, 0))],
        out_specs=pl.BlockSpec((None, tm, n), lambda i, j: (i, j, 0)),
        out_shape=jax.ShapeDtypeStruct((b, s, n), BF16),
        compiler_params=_cparams(2),
        name="norm1_inproj",
    )(x, mod_l, g1, w_in_bf, gq_t, gk_t, gmat)


def _head_masked(q):
    lane = lax.broadcasted_iota(jnp.int32, q.shape, 1)
    zero = jnp.zeros_like(q)
    return jnp.where(lane < HEAD_DIM, q, zero), jnp.where(lane >= HEAD_DIM, q, zero)


def _scores_t(k_blk, q_masked):
    return lax.dot_general(k_blk, q_masked, (((1,), (1,)), ((), ())), preferred_element_type=F32)


def _fill_vt(v_ref, vt_ref, blk):
    for i in range(vt_ref.shape[0]):
        vt_ref[i] = v_ref[i * blk:(i + 1) * blk, :].astype(F32).T.astype(BF16)


def _merge_heads_t(out_a, out_b):
    row = lax.broadcasted_iota(jnp.int32, out_a.shape, 0)
    return jnp.where(row < HEAD_DIM, out_a, out_b).T


def _split_bf16(x):
    hi = lax.bitcast_convert_type(lax.bitcast_convert_type(x, jnp.uint32) & jnp.uint32(0xFFFF0000), F32)
    return hi.astype(BF16), (x - hi).astype(BF16)


def _sb_kernel(q_ref, k_ref, v_ref, o_ref, vt_ref, acc_ref):
    tq = vt_ref.shape[2]
    tk = tq
    n_tiles = q_ref.shape[0] // tq
    qi = pl.program_id(2)

    @pl.when(qi == 0)
    def _():
        _fill_vt(v_ref, vt_ref, tk)

    q_heads = [_head_masked(q_ref[t * tq:(t + 1) * tq, :]) for t in range(n_tiles)]
    row = lax.broadcasted_iota(jnp.int32, (tk, tq), 0)
    lane = lax.broadcasted_iota(jnp.int32, (tk, tq), 1)
    tri = (lane > row).astype(BF16)
    strict = row < lane

    def blocks(work, carries, first):
        depth = max(len(kbs) for kbs in work.values())
        chains = [(t, n, h) for n in range(depth) for t in work if n < len(work[t]) for h in range(2)]
        k_blk = lambda kb: k_ref[pl.ds(pl.multiple_of(kb * tk, tk), tk), :]
        z = {(t, n, h): _scores_t(k_blk(work[t][n]), q_heads[t][h]) for t, n, h in chains}
        log_beta, suffix, total = {}, {}, {}
        for c in chains:
            sp = jnp.maximum(z[c], 0.0) + jnp.log(1.0 + jnp.exp2(jnp.abs(z[c]) * (-LOG2E)))
            log_beta[c] = z[c] - sp
            if first and c[1] == 0:
                sp = jnp.where(strict, sp, 0.0)
            hi, lo = _split_bf16(sp)
            suffix[c] = (jnp.dot(tri, hi, preferred_element_type=F32) +
                         jnp.dot(tri, lo, preferred_element_type=F32))
            total[c] = suffix[c][0:1, :] + sp[0:1, :]
        carries = {t: list(carries[t]) for t in work}
        pv = {}
        for c in chains:
            t, n, h = c
            w = jnp.exp(log_beta[c] - suffix[c] + carries[t][h])
            if first and n == 0:
                w = jnp.where(strict, w, 0.0)
            out = jnp.dot(vt_ref[work[t][n]], w.astype(BF16), preferred_element_type=F32)
            pv[t, h] = out if (t, h) not in pv else pv[t, h] + out
            carries[t][h] = carries[t][h] - total[c]
        for t, h in pv:
            if first:
                acc_ref[t, h] = pv[t, h]
            else:
                acc_ref[t, h] += pv[t, h]
        return tuple(tuple(carries[t]) for t in work)

    zero = jnp.zeros((1, tq), F32)
    zeros = {t: (zero, zero) for t in range(n_tiles)}
    base = qi * n_tiles
    full = {t: [base + t, base + t - 1] for t in range(n_tiles)}
    head = dict(full)
    head[0] = [base]
    carries = lax.cond(qi == 0,
                       lambda: blocks(head, zeros, first=True),
                       lambda: blocks(full, zeros, first=True))

    def cond(state):
        kb, ca, cb = state
        alive = jnp.max(jnp.maximum(ca, cb)) > SB_DEAD
        return jnp.logical_and(kb >= 0, alive)

    for t in range(n_tiles):
        def body(state, t=t):
            kb, ca, cb = state
            ((ca, cb),) = blocks({t: [kb]}, {t: (ca, cb)}, first=False)
            return kb - 1, ca, cb

        lax.while_loop(cond, body, (base + t - 2,) + carries[t])
        o_ref[t * tq:(t + 1) * tq, :] = _merge_heads_t(acc_ref[t, 0], acc_ref[t, 1]).astype(o_ref.dtype)


def _sb_attention(proj):
    b, s, _ = proj.shape
    tq = min(SB_TQ, s)
    step = min(SB_TILES * tq, s)
    return pl.pallas_call(
        _sb_kernel,
        grid=(b, PAIRS, s // step),
        in_specs=[pl.BlockSpec((None, step, LANES), lambda i, p, j: (i, j, p)),
                  pl.BlockSpec((None, s, LANES), lambda i, p, j: (i, 0, PAIRS + p)),
                  pl.BlockSpec((None, s, LANES), lambda i, p, j: (i, 0, 2 * PAIRS + p))],
        out_specs=pl.BlockSpec((None, step, LANES), lambda i, p, j: (i, j, p)),
        out_shape=jax.ShapeDtypeStruct((b, s, D_GROUP), BF16),
        scratch_shapes=[pltpu.VMEM((s // tq, LANES, tq), BF16),
                        pltpu.VMEM((step // tq, 2, LANES, tq), F32)],
        compiler_params=_cparams(3),
        name="stickbreak_attn",
    )(proj, proj, proj)


def _ca_kernel(q_ref, k_ref, v_ref, bias_ref, o_ref, vt_ref):
    tq = vt_ref.shape[2]
    n_tiles = q_ref.shape[0] // tq
    n_win = CA_WIN // tq
    qi = pl.program_id(2)

    @pl.when(qi == 0)
    def _():
        _fill_vt(v_ref, vt_ref, tq)

    units = [(t, h) for t in range(n_tiles) for h in range(2)]
    q_heads = [_head_masked(q_ref[t * tq:(t + 1) * tq, :]) for t in range(n_tiles)]
    first_blk = [qi * n_tiles + t - (n_win - 1) for t in range(n_tiles)]
    blks = [[jnp.maximum(first_blk[t] + i, 0) for i in range(n_win)] for t in range(n_tiles)]
    k_blk = lambda bi: k_ref[pl.ds(pl.multiple_of(bi * tq, tq), tq), :]
    raw = {(t, h): [_scores_t(k_blk(blks[t][i]), q_heads[t][h]) for i in range(n_win)] for t, h in units}
    acc, denom = {}, {}
    for t, h in units:
        s_blks = []
        for i in range(n_win):
            sc = raw[t, h][i] + bias_ref[h, i * tq:(i + 1) * tq, :]
            s_blks.append(jnp.where(first_blk[t] + i >= 0, sc, NEG_INF))
        m = s_blks[0].max(axis=0, keepdims=True)
        for sc in s_blks[1:]:
            m = jnp.maximum(m, sc.max(axis=0, keepdims=True))
        denom[t, h] = jnp.zeros((1, tq), F32)
        for i in range(n_win):
            p = jnp.exp(s_blks[i] - m)
            denom[t, h] = denom[t, h] + p.sum(axis=0, keepdims=True)
            out = jnp.dot(vt_ref[blks[t][i]], p.astype(BF16), preferred_element_type=F32)
            acc[t, h] = out if i == 0 else acc[t, h] + out
    for t in range(n_tiles):
        outs = [acc[t, h] * (1.0 / denom[t, h]) for h in range(2)]
        o_ref[t * tq:(t + 1) * tq, :] = _merge_heads_t(outs[0], outs[1]).astype(o_ref.dtype)


def _ca_attention(proj, bias_tab):
    b, s, _ = proj.shape
    tq = CA_TQ
    step = min(CA_TILES * tq, s)
    base = 3 * PAIRS
    return pl.pallas_call(
        _ca_kernel,
        grid=(b, PAIRS, s // step),
        in_specs=[pl.BlockSpec((None, step, LANES), lambda i, p, j: (i, j, base + p)),
                  pl.BlockSpec((None, s, LANES), lambda i, p, j: (i, 0, base + PAIRS + p)),
                  pl.BlockSpec((None, s, LANES), lambda i, p, j: (i, 0, base + 2 * PAIRS + p)),
                  pl.BlockSpec((2, CA_WIN, tq), lambda i, p, j: (p, 0, 0))],
        out_specs=pl.BlockSpec((None, step, LANES), lambda i, p, j: (i, j, p)),
        out_shape=jax.ShapeDtypeStruct((b, s, D_GROUP), BF16),
        scratch_shapes=[pltpu.VMEM((s // tq, LANES, tq), BF16)],
        compiler_params=_cparams(3),
        name="chunkrel_attn",
    )(proj, proj, proj, bias_tab)


BIAS_ROWS = 128
BIAS_EXT = CA_WIN + CA_TQ


def _ca_bias_kernel(ext_ref, o_ref):
    width = CA_TQ + BIAS_ROWS
    r_loc = lax.broadcasted_iota(jnp.int32, (BIAS_ROWS, CA_TQ), 0)
    col = lax.broadcasted_iota(jnp.int32, (BIAS_ROWS, CA_TQ), 1)
    chunk_start = (col // CHUNK) * CHUNK
    for a in range(CA_WIN // BIAS_ROWS):
        start = CA_WIN - BIAS_ROWS * (a + 1)
        m = jnp.broadcast_to(ext_ref[:, start:start + width], (BIAS_ROWS, width))
        m = pltpu.roll(m, 0, 1, stride=1, stride_axis=0)
        k_in_band = r_loc + (a * BIAS_ROWS) - chunk_start
        in_band = jnp.logical_and(k_in_band >= 0, k_in_band < BAND)
        o_ref[a * BIAS_ROWS:(a + 1) * BIAS_ROWS, :] = jnp.where(
            in_band, m[:, BIAS_ROWS:BIAS_ROWS + CA_TQ], NEG_INF)


def _ca_bias_table(rel_bias):
    depth, n_heads, n_rel = rel_bias.shape
    rb = rel_bias.reshape(depth * n_heads, n_rel).astype(F32)
    n_lo = CA_TQ - REL_CLIP
    n_hi = BIAS_EXT - n_lo - n_rel
    ext = jnp.concatenate([jnp.broadcast_to(rb[:, :1], (rb.shape[0], n_lo)), rb,
                           jnp.broadcast_to(rb[:, -1:], (rb.shape[0], n_hi))], axis=1)
    return pl.pallas_call(
        _ca_bias_kernel,
        grid=(depth * n_heads,),
        in_specs=[pl.BlockSpec((None, 1, BIAS_EXT), lambda i: (i, 0, 0))],
        out_specs=pl.BlockSpec((None, CA_WIN, CA_TQ), lambda i: (i, 0, 0)),
        out_shape=jax.ShapeDtypeStruct((depth * n_heads, CA_WIN, CA_TQ), F32),
        compiler_params=_cparams(1),
        name="ca_bias_table",
    )(ext.reshape(depth * n_heads, 1, BIAS_EXT))


def _mlp_kernel(x_ref, osb_ref, oca_ref, mod_ref, g2_ref, wo_ref, w1_ref, w2_ref, o_ref):
    att = (jnp.dot(osb_ref[...], wo_ref[0:D_GROUP, :], preferred_element_type=F32) +
           jnp.dot(oca_ref[...], wo_ref[D_GROUP:, :], preferred_element_type=F32))
    x1 = x_ref[...] + mod_ref[2:3, :] * att
    h = _modulated_norm(x1, g2_ref[...], mod_ref[3:4, :], mod_ref[4:5, :]).astype(BF16)
    d_ff = w1_ref.shape[1]
    fc = 1024
    acc = jnp.zeros(x1.shape, F32)
    for c in range(d_ff // fc):
        u = jnp.maximum(jnp.dot(h, w1_ref[:, c * fc:(c + 1) * fc], preferred_element_type=F32), 0.0)
        acc = acc + jnp.dot((u * u).astype(BF16), w2_ref[c * fc:(c + 1) * fc, :],
                            preferred_element_type=F32)
    o_ref[...] = x1 + mod_ref[5:6, :] * acc


def _outproj_mlp(x, o_sb, o_ca, mod_l, g2, wo_bf, w1_bf, w2_bf):
    b, s, d = x.shape
    d_ff = w1_bf.shape[1]
    tm = min(ROW_TILE, s)
    return pl.pallas_call(
        _mlp_kernel,
        grid=(b, s // tm),
        in_specs=[pl.BlockSpec((None, tm, d), lambda i, j: (i, j, 0)),
                  pl.BlockSpec((None, tm, D_GROUP), lambda i, j: (i, j, 0)),
                  pl.BlockSpec((None, tm, D_GROUP), lambda i, j: (i, j, 0)),
                  pl.BlockSpec((None, 6, d), lambda i, j: (i, 0, 0)),
                  _resident((1, d), lambda i, j: (0, 0)),
                  _resident((d, d), lambda i, j: (0, 0)),
                  _resident((d, d_ff), lambda i, j: (0, 0)),
                  _resident((d_ff, d), lambda i, j: (0, 0))],
        out_specs=pl.BlockSpec((None, tm, d), lambda i, j: (i, j, 0)),
        out_shape=jax.ShapeDtypeStruct((b, s, d), F32),
        compiler_params=_cparams(2),
        name="outproj_mlp",
    )(x, o_sb, o_ca, mod_l, g2, wo_bf, w1_bf, w2_bf)


def kernel(x, c, g_norm1, w_in, g_q, g_k, rel_bias, w_o, g_norm2, w1, w2, w_ada, b_ada):
    depth = w_in.shape[0]
    mod = _ada_modulation(c, w_ada, b_ada)
    bias_tab = _ca_bias_table(rel_bias)
    lane_head = jnp.arange(D_GROUP) // HEAD_DIM
    gmat = ((lane_head[:, None] == lane_head[None, :]).astype(F32) * (1.0 / HEAD_DIM)).astype(BF16)
    for l in range(depth):
        gq_t = (jnp.tile(g_q[l], HEADS_PER_GROUP) * QK_SCALE)[None, :]
        gk_t = jnp.tile(g_k[l], HEADS_PER_GROUP)[None, :]
        proj = _inproj(x, mod[l], g_norm1[l][None, :], w_in[l].astype(BF16), gq_t, gk_t, gmat)
        o_sb = _sb_attention(proj)
        o_ca = _ca_attention(proj, bias_tab[l * HEADS_PER_GROUP:(l + 1) * HEADS_PER_GROUP])
        x = _outproj_mlp(x, o_sb, o_ca, mod[l], g_norm2[l][None, :],
                         w_o[l].astype(BF16), w1[l].astype(BF16), w2[l].astype(BF16))
    return x
```

```python
import jax
import jax.numpy as jnp
from jax import lax
from jax.experimental import pallas as pl
from jax.experimental.pallas import tpu as pltpu

F32 = jnp.float32
BF16 = jnp.bfloat16

HEAD_DIM = 64
LANES = 128
HEADS_PER_GROUP = 8
PAIRS = HEADS_PER_GROUP // 2
D_GROUP = HEADS_PER_GROUP * HEAD_DIM
CHUNK = 64
LEFT_CHUNKS = 8
BAND = (LEFT_CHUNKS + 1) * CHUNK
REL_CLIP = 128
EPS = 1e-6
NEG_INF = -1e30
QK_SCALE = HEAD_DIM ** -0.5
LOG2E = 1.4426950408889634

ROW_TILE = 512
SB_TQ = 256
SB_TILES = 4
CA_TQ = 256
CA_TILES = 4
CA_WIN = CA_TQ + LEFT_CHUNKS * CHUNK
SB_DEAD = -106.0
VMEM_LIMIT = 56 * 1024 * 1024


def _cparams(n_axes):
    return pltpu.CompilerParams(dimension_semantics=("arbitrary",) * n_axes,
                                vmem_limit_bytes=VMEM_LIMIT)


def _resident(shape, index_map):
    return pl.BlockSpec(shape, index_map, pipeline_mode=pl.Buffered(1))


def _ada_kernel(c_ref, w_ref, b_ref, o_ref):
    ca = jax.nn.silu(c_ref[...]).astype(BF16)
    o_ref[...] = jnp.dot(ca, w_ref[...].astype(BF16), preferred_element_type=F32) + b_ref[...]


def _ada_modulation(c, w_ada, b_ada):
    depth, d, n = w_ada.shape
    b = c.shape[0]
    rows = 8
    c_pad = jnp.pad(c, ((0, rows - b), (0, 0)))
    tn = d
    out = pl.pallas_call(
        _ada_kernel,
        grid=(depth, n // tn),
        in_specs=[pl.BlockSpec((rows, d), lambda l, j: (0, 0)),
                  pl.BlockSpec((None, d, tn), lambda l, j: (l, 0, j)),
                  pl.BlockSpec((None, 1, tn), lambda l, j: (l, 0, j))],
        out_specs=pl.BlockSpec((None, rows, tn), lambda l, j: (l, 0, j)),
        out_shape=jax.ShapeDtypeStruct((depth, rows, n), F32),
        compiler_params=_cparams(2),
        name="ada_modulation",
    )(c_pad, w_ada, b_ada.reshape(depth, 1, n))
    return out[:, :b].reshape(depth, b, 6, d)


def _modulated_norm(x, g, shift, scale):
    ms = jnp.mean(x * x, axis=-1, keepdims=True)
    return (x * lax.rsqrt(ms + EPS) * g) * (1.0 + scale) + shift


def _inproj_kernel(x_ref, mod_ref, g1_ref, w_ref, gq_ref, gk_ref, gmat_ref, o_ref):
    h = _modulated_norm(x_ref[...], g1_ref[...], mod_ref[0:1, :], mod_ref[1:2, :]).astype(BF16)
    for c in range(6):
        cols = slice(c * D_GROUP, (c + 1) * D_GROUP)
        y = jnp.dot(h, w_ref[:, cols], preferred_element_type=F32)
        if c == 0:
            y = y * QK_SCALE
        elif c in (3, 4):
            msq = jnp.dot((y * y).astype(BF16), gmat_ref[...], preferred_element_type=F32)
            y = y * lax.rsqrt(msq + EPS) * (gq_ref[...] if c == 3 else gk_ref[...])
        o_ref[:, cols] = y.astype(BF16)


def _inproj(x, mod, layer, g1, w_in_bf, gq_t, gk_t, gmat):
    b, s, d = x.shape
    n = w_in_bf.shape[1]
    tm = min(ROW_TILE, s)
    return pl.pallas_call(
        _inproj_kernel,
        grid=(b, s // tm),
        in_specs=[pl.BlockSpec((None, tm, d), lambda i, j: (i, j, 0)),
                  pl.BlockSpec((None, None, 6, d), lambda i, j: (layer, i, 0, 0)),
                  _resident((1, d), lambda i, j: (0, 0)),
                  _resident((d, n), lambda i, j: (0, 0)),
                  _resident((1, D_GROUP), lambda i, j: (0, 0)),
                  _resident((1, D_GROUP), lambda i, j: (0, 0)),
                  _resident((D_GROUP, D_GROUP), lambda i, j: (0, 0))],
        out_specs=pl.BlockSpec((None, tm, n), lambda i, j: (i, j, 0)),
        out_shape=jax.ShapeDtypeStruct((b, s, n), BF16),
        compiler_params=_cparams(2),
        name="norm1_inproj",
    )(x, mod, g1, w_in_bf, gq_t, gk_t, gmat)


def _head_masked(q):
    lane = lax.broadcasted_iota(jnp.int32, q.shape, 1)
    zero = jnp.zeros_like(q)
    return jnp.where(lane < HEAD_DIM, q, zero), jnp.where(lane >= HEAD_DIM, q, zero)


def _scores_t(k_blk, q_masked):
    return lax.dot_general(k_blk, q_masked, (((1,), (1,)), ((), ())), preferred_element_type=F32)


def _fill_vt(v_ref, vt_ref, blk):
    for i in range(vt_ref.shape[0]):
        vt_ref[i] = v_ref[i * blk:(i + 1) * blk, :].astype(F32).T.astype(BF16)


def _merge_heads_t(out_a, out_b):
    row = lax.broadcasted_iota(jnp.int32, out_a.shape, 0)
    return jnp.where(row < HEAD_DIM, out_a, out_b).T


def _split_bf16(x):
    hi = lax.bitcast_convert_type(lax.bitcast_convert_type(x, jnp.uint32) & jnp.uint32(0xFFFF0000), F32)
    return hi.astype(BF16), (x - hi).astype(BF16)


def _sb_kernel(q_ref, k_ref, v_ref, o_ref, vt_ref, acc_ref):
    tq = vt_ref.shape[2]
    tk = tq
    n_tiles = q_ref.shape[0] // tq
    qi = pl.program_id(2)

    @pl.when(qi == 0)
    def _():
        _fill_vt(v_ref, vt_ref, tk)

    q_heads = [_head_masked(q_ref[t * tq:(t + 1) * tq, :]) for t in range(n_tiles)]
    row = lax.broadcasted_iota(jnp.int32, (tk, tq), 0)
    lane = lax.broadcasted_iota(jnp.int32, (tk, tq), 1)
    tri = (lane >= row).astype(BF16)
    strict = row < lane

    def blocks(work, carries, first):
        depth = max(len(kbs) for kbs in work.values())
        chains = [(t, n, h) for n in range(depth) for t in work if n < len(work[t]) for h in range(2)]
        k_blk = lambda kb: k_ref[pl.ds(pl.multiple_of(kb * tk, tk), tk), :]
        z = {(t, n, h): _scores_t(k_blk(work[t][n]), q_heads[t][h]) for t, n, h in chains}
        incl = {}
        for c in chains:
            if first and c[1] == 0:
                z[c] = jnp.where(strict, z[c], NEG_INF)
            sp = jnp.maximum(z[c], 0.0) + jnp.log(1.0 + jnp.exp2(jnp.abs(z[c]) * (-LOG2E)))
            hi, lo = _split_bf16(sp)
            incl[c] = (jnp.dot(tri, hi, preferred_element_type=F32) +
                       jnp.dot(tri, lo, preferred_element_type=F32))
        carries = {t: list(carries[t]) for t in work}
        pv = {}
        for c in chains:
            t, n, h = c
            log_w = z[c] - incl[c]
            if not (first and n == 0):
                log_w = log_w + carries[t][h]
            out = jnp.dot(vt_ref[work[t][n]], jnp.exp(log_w).astype(BF16), preferred_element_type=F32)
            pv[t, h] = out if (t, h) not in pv else pv[t, h] + out
            carries[t][h] = carries[t][h] - incl[c][0:1, :]
        for t, h in pv:
            if first:
                acc_ref[t, h] = pv[t, h]
            else:
                acc_ref[t, h] += pv[t, h]
        return tuple(tuple(carries[t]) for t in work)

    zero = jnp.zeros((1, tq), F32)
    zeros = {t: (zero, zero) for t in range(n_tiles)}
    base = qi * n_tiles
    full = {t: [base + t, base + t - 1] for t in range(n_tiles)}
    head = dict(full)
    head[0] = [base]
    carries = lax.cond(qi == 0,
                       lambda: blocks(head, zeros, first=True),
                       lambda: blocks(full, zeros, first=True))

    def cond(state):
        kb, ca, cb = state
        alive = jnp.max(jnp.maximum(ca, cb)) > SB_DEAD
        return jnp.logical_and(kb >= 0, alive)

    least_dead = carries[0][0]
    for t in range(n_tiles):
        for h in range(2):
            least_dead = jnp.maximum(least_dead, carries[t][h])

    @pl.when(jnp.max(least_dead) > SB_DEAD)
    def _():
        for t in range(n_tiles):
            def body(state, t=t):
                kb, ca, cb = state
                ((ca, cb),) = blocks({t: [kb]}, {t: (ca, cb)}, first=False)
                return kb - 1, ca, cb

            lax.while_loop(cond, body, (base + t - 2,) + carries[t])

    for t in range(n_tiles):
        o_ref[t * tq:(t + 1) * tq, :] = _merge_heads_t(acc_ref[t, 0], acc_ref[t, 1]).astype(o_ref.dtype)


def _sb_attention(proj):
    b, s, _ = proj.shape
    tq = min(SB_TQ, s)
    step = min(SB_TILES * tq, s)
    return pl.pallas_call(
        _sb_kernel,
        grid=(b, PAIRS, s // step),
        in_specs=[pl.BlockSpec((None, step, LANES), lambda i, p, j: (i, j, p)),
                  pl.BlockSpec((None, s, LANES), lambda i, p, j: (i, 0, PAIRS + p)),
                  pl.BlockSpec((None, s, LANES), lambda i, p, j: (i, 0, 2 * PAIRS + p))],
        out_specs=pl.BlockSpec((None, step, LANES), lambda i, p, j: (i, j, p)),
        out_shape=jax.ShapeDtypeStruct((b, s, D_GROUP), BF16),
        scratch_shapes=[pltpu.VMEM((s // tq, LANES, tq), BF16),
                        pltpu.VMEM((step // tq, 2, LANES, tq), F32)],
        compiler_params=_cparams(3),
        name="stickbreak_attn",
    )(proj, proj, proj)


def _ca_kernel(q_ref, k_ref, v_ref, bias_ref, o_ref, vt_ref):
    tq = vt_ref.shape[2]
    n_tiles = q_ref.shape[0] // tq
    n_win = CA_WIN // tq
    qi = pl.program_id(2)

    @pl.when(qi == 0)
    def _():
        _fill_vt(v_ref, vt_ref, tq)

    units = [(t, h) for t in range(n_tiles) for h in range(2)]
    q_heads = [_head_masked(q_ref[t * tq:(t + 1) * tq, :]) for t in range(n_tiles)]
    first_blk = [qi * n_tiles + t - (n_win - 1) for t in range(n_tiles)]
    blks = [[jnp.maximum(first_blk[t] + i, 0) for i in range(n_win)] for t in range(n_tiles)]
    k_blk = lambda bi: k_ref[pl.ds(pl.multiple_of(bi * tq, tq), tq), :]

    def attend(sequence_start):
        raw = {(t, h): [_scores_t(k_blk(blks[t][i]), q_heads[t][h]) for i in range(n_win)]
               for t, h in units}
        acc, denom = {}, {}
        for t, h in units:
            s_blks = []
            for i in range(n_win):
                sc = raw[t, h][i] + bias_ref[h, i * tq:(i + 1) * tq, :]
                if sequence_start:
                    sc = jnp.where(first_blk[t] + i >= 0, sc, NEG_INF)
                s_blks.append(sc)
            m = s_blks[0].max(axis=0, keepdims=True)
            for sc in s_blks[1:]:
                m = jnp.maximum(m, sc.max(axis=0, keepdims=True))
            denom[t, h] = jnp.zeros((1, tq), F32)
            for i in range(n_win):
                p = jnp.exp(s_blks[i] - m)
                denom[t, h] = denom[t, h] + p.sum(axis=0, keepdims=True)
                out = jnp.dot(vt_ref[blks[t][i]], p.astype(BF16), preferred_element_type=F32)
                acc[t, h] = out if i == 0 else acc[t, h] + out
        for t in range(n_tiles):
            outs = [acc[t, h] * (1.0 / denom[t, h]) for h in range(2)]
            o_ref[t * tq:(t + 1) * tq, :] = _merge_heads_t(outs[0], outs[1]).astype(o_ref.dtype)

    n_start_steps = -(-(n_win - 1) // n_tiles)
    pl.when(qi < n_start_steps)(lambda: attend(True))
    pl.when(qi >= n_start_steps)(lambda: attend(False))


def _ca_attention(proj, bias_tab, layer):
    b, s, _ = proj.shape
    tq = CA_TQ
    step = min(CA_TILES * tq, s)
    base = 3 * PAIRS
    return pl.pallas_call(
        _ca_kernel,
        grid=(b, PAIRS, s // step),
        in_specs=[pl.BlockSpec((None, step, LANES), lambda i, p, j: (i, j, base + p)),
                  pl.BlockSpec((None, s, LANES), lambda i, p, j: (i, 0, base + PAIRS + p)),
                  pl.BlockSpec((None, s, LANES), lambda i, p, j: (i, 0, base + 2 * PAIRS + p)),
                  pl.BlockSpec((2, CA_WIN, tq), lambda i, p, j: (layer * PAIRS + p, 0, 0))],
        out_specs=pl.BlockSpec((None, step, LANES), lambda i, p, j: (i, j, p)),
        out_shape=jax.ShapeDtypeStruct((b, s, D_GROUP), BF16),
        scratch_shapes=[pltpu.VMEM((s // tq, LANES, tq), BF16)],
        compiler_params=_cparams(3),
        name="chunkrel_attn",
    )(proj, proj, proj, bias_tab)


BIAS_ROWS = 128
BIAS_EXT = CA_WIN + CA_TQ


def _ca_bias_kernel(ext_ref, o_ref):
    width = CA_TQ + BIAS_ROWS
    r_loc = lax.broadcasted_iota(jnp.int32, (BIAS_ROWS, CA_TQ), 0)
    col = lax.broadcasted_iota(jnp.int32, (BIAS_ROWS, CA_TQ), 1)
    chunk_start = (col // CHUNK) * CHUNK
    for a in range(CA_WIN // BIAS_ROWS):
        start = CA_WIN - BIAS_ROWS * (a + 1)
        m = jnp.broadcast_to(ext_ref[:, start:start + width], (BIAS_ROWS, width))
        m = pltpu.roll(m, 0, 1, stride=1, stride_axis=0)
        k_in_band = r_loc + (a * BIAS_ROWS) - chunk_start
        in_band = jnp.logical_and(k_in_band >= 0, k_in_band < BAND)
        o_ref[a * BIAS_ROWS:(a + 1) * BIAS_ROWS, :] = jnp.where(
            in_band, m[:, BIAS_ROWS:BIAS_ROWS + CA_TQ], NEG_INF)


def _ca_bias_table(rel_bias):
    depth, n_heads, n_rel = rel_bias.shape
    rb = rel_bias.reshape(depth * n_heads, n_rel).astype(F32)
    n_lo = CA_TQ - REL_CLIP
    n_hi = BIAS_EXT - n_lo - n_rel
    ext = jnp.concatenate([jnp.broadcast_to(rb[:, :1], (rb.shape[0], n_lo)), rb,
                           jnp.broadcast_to(rb[:, -1:], (rb.shape[0], n_hi))], axis=1)
    return pl.pallas_call(
        _ca_bias_kernel,
        grid=(depth * n_heads,),
        in_specs=[pl.BlockSpec((None, 1, BIAS_EXT), lambda i: (i, 0, 0))],
        out_specs=pl.BlockSpec((None, CA_WIN, CA_TQ), lambda i: (i, 0, 0)),
        out_shape=jax.ShapeDtypeStruct((depth * n_heads, CA_WIN, CA_TQ), F32),
        compiler_params=_cparams(1),
        name="ca_bias_table",
    )(ext.reshape(depth * n_heads, 1, BIAS_EXT))


def _mlp_kernel(x_ref, osb_ref, oca_ref, mod_ref, g2_ref, wo_ref, w1_ref, w2_ref, o_ref):
    att = (jnp.dot(osb_ref[...], wo_ref[0:D_GROUP, :], preferred_element_type=F32) +
           jnp.dot(oca_ref[...], wo_ref[D_GROUP:, :], preferred_element_type=F32))
    x1 = x_ref[...] + mod_ref[2:3, :] * att
    h = _modulated_norm(x1, g2_ref[...], mod_ref[3:4, :], mod_ref[4:5, :]).astype(BF16)
    d_ff = w1_ref.shape[1]
    fc = 1024
    acc = jnp.zeros(x1.shape, F32)
    for c in range(d_ff // fc):
        u = jnp.maximum(jnp.dot(h, w1_ref[:, c * fc:(c + 1) * fc], preferred_element_type=F32), 0.0)
        acc = acc + jnp.dot((u * u).astype(BF16), w2_ref[c * fc:(c + 1) * fc, :],
                            preferred_element_type=F32)
    o_ref[...] = x1 + mod_ref[5:6, :] * acc


def _outproj_mlp(x, o_sb, o_ca, mod, layer, g2, wo_bf, w1_bf, w2_bf):
    b, s, d = x.shape
    d_ff = w1_bf.shape[1]
    tm = min(ROW_TILE, s)
    return pl.pallas_call(
        _mlp_kernel,
        grid=(b, s // tm),
        in_specs=[pl.BlockSpec((None, tm, d), lambda i, j: (i, j, 0)),
                  pl.BlockSpec((None, tm, D_GROUP), lambda i, j: (i, j, 0)),
                  pl.BlockSpec((None, tm, D_GROUP), lambda i, j: (i, j, 0)),
                  pl.BlockSpec((None, None, 6, d), lambda i, j: (layer, i, 0, 0)),
                  _resident((1, d), lambda i, j: (0, 0)),
                  _resident((d, d), lambda i, j: (0, 0)),
                  _resident((d, d_ff), lambda i, j: (0, 0)),
                  _resident((d_ff, d), lambda i, j: (0, 0))],
        out_specs=pl.BlockSpec((None, tm, d), lambda i, j: (i, j, 0)),
        out_shape=jax.ShapeDtypeStruct((b, s, d), F32),
        compiler_params=_cparams(2),
        name="outproj_mlp",
    )(x, o_sb, o_ca, mod, g2, wo_bf, w1_bf, w2_bf)


def kernel(x, c, g_norm1, w_in, g_q, g_k, rel_bias, w_o, g_norm2, w1, w2, w_ada, b_ada):
    depth = w_in.shape[0]
    mod = _ada_modulation(c, w_ada, b_ada)
    bias_tab = _ca_bias_table(rel_bias)
    lane_head = jnp.arange(D_GROUP) // HEAD_DIM
    gmat = ((lane_head[:, None] == lane_head[None, :]).astype(F32) * (1.0 / HEAD_DIM)).astype(BF16)
    for l in range(depth):
        gq_t = (jnp.tile(g_q[l], HEADS_PER_GROUP) * QK_SCALE)[None, :]
        gk_t = jnp.tile(g_k[l], HEADS_PER_GROUP)[None, :]
        proj = _inproj(x, mod, l, g_norm1[l][None, :], w_in[l].astype(BF16), gq_t, gk_t, gmat)
        o_sb = _sb_attention(proj)
        o_ca = _ca_attention(proj, bias_tab, l)
        x = _outproj_mlp(x, o_sb, o_ca, mod, l, g_norm2[l][None, :],
                         w_o[l].astype(BF16), w1[l].astype(BF16), w2[l].astype(BF16))
    return x
```

```python
import jax
import jax.numpy as jnp
from jax import lax
from jax.experimental import pallas as pl
from jax.experimental.pallas import tpu as pltpu

F32 = jnp.float32
BF16 = jnp.bfloat16

HEAD_DIM = 64
LANES = 128
HEADS_PER_GROUP = 8
PAIRS = HEADS_PER_GROUP // 2
D_GROUP = HEADS_PER_GROUP * HEAD_DIM
CHUNK = 64
LEFT_CHUNKS = 8
BAND = (LEFT_CHUNKS + 1) * CHUNK
REL_CLIP = 128
EPS = 1e-6
NEG_INF = -1e30
QK_SCALE = HEAD_DIM ** -0.5
LOG2E = 1.4426950408889634

ROW_TILE = 512
SB_TQ = 256
SB_TILES = 4
CA_TQ = 256
CA_TILES = 4
CA_WIN = CA_TQ + LEFT_CHUNKS * CHUNK
SB_DEAD = -106.0
VMEM_LIMIT = 56 * 1024 * 1024


def _cparams(n_axes):
    return pltpu.CompilerParams(dimension_semantics=("arbitrary",) * n_axes,
                                vmem_limit_bytes=VMEM_LIMIT)


def _resident(shape, index_map):
    return pl.BlockSpec(shape, index_map, pipeline_mode=pl.Buffered(1))


def _ada_kernel(c_ref, w_ref, b_ref, o_ref):
    ca = jax.nn.silu(c_ref[...]).astype(BF16)
    o_ref[...] = jnp.dot(ca, w_ref[...].astype(BF16), preferred_element_type=F32) + b_ref[...]


def _ada_modulation(c, w_ada, b_ada):
    depth, d, n = w_ada.shape
    b = c.shape[0]
    rows = 8
    c_pad = jnp.pad(c, ((0, rows - b), (0, 0)))
    tn = d
    out = pl.pallas_call(
        _ada_kernel,
        grid=(depth, n // tn),
        in_specs=[pl.BlockSpec((rows, d), lambda l, j: (0, 0)),
                  pl.BlockSpec((None, d, tn), lambda l, j: (l, 0, j)),
                  pl.BlockSpec((None, 1, tn), lambda l, j: (l, 0, j))],
        out_specs=pl.BlockSpec((None, rows, tn), lambda l, j: (l, 0, j)),
        out_shape=jax.ShapeDtypeStruct((depth, rows, n), F32),
        compiler_params=_cparams(2),
        name="ada_modulation",
    )(c_pad, w_ada, b_ada.reshape(depth, 1, n))
    return out[:, :b].reshape(depth, b, 6, d)


def _modulated_norm(x, g, shift, scale):
    ms = jnp.mean(x * x, axis=-1, keepdims=True)
    return (x * lax.rsqrt(ms + EPS) * g) * (1.0 + scale) + shift


def _inproj_kernel(x_ref, mod_ref, g1_ref, w_ref, gq_ref, gk_ref, gmat_ref, o_ref):
    h = _modulated_norm(x_ref[...], g1_ref[...], mod_ref[0:1, :], mod_ref[1:2, :]).astype(BF16)
    for c in range(6):
        cols = slice(c * D_GROUP, (c + 1) * D_GROUP)
        y = jnp.dot(h, w_ref[:, cols], preferred_element_type=F32)
        if c == 0:
            y = y * QK_SCALE
        elif c in (3, 4):
            msq = jnp.dot((y * y).astype(BF16), gmat_ref[...], preferred_element_type=F32)
            y = y * lax.rsqrt(msq + EPS) * (gq_ref[...] if c == 3 else gk_ref[...])
        o_ref[:, cols] = y.astype(BF16)


def _inproj(x, mod, layer, g1, w_in_bf, gq_t, gk_t, gmat):
    b, s, d = x.shape
    n = w_in_bf.shape[1]
    tm = min(ROW_TILE, s)
    return pl.pallas_call(
        _inproj_kernel,
        grid=(b, s // tm),
        in_specs=[pl.BlockSpec((None, tm, d), lambda i, j: (i, j, 0)),
                  pl.BlockSpec((None, None, 6, d), lambda i, j: (layer, i, 0, 0)),
                  _resident((1, d), lambda i, j: (0, 0)),
                  _resident((d, n), lambda i, j: (0, 0)),
                  _resident((1, D_GROUP), lambda i, j: (0, 0)),
                  _resident((1, D_GROUP), lambda i, j: (0, 0)),
                  _resident((D_GROUP, D_GROUP), lambda i, j: (0, 0))],
        out_specs=pl.BlockSpec((None, tm, n), lambda i, j: (i, j, 0)),
        out_shape=jax.ShapeDtypeStruct((b, s, n), BF16),
        compiler_params=_cparams(2),
        name="norm1_inproj",
    )(x, mod, g1, w_in_bf, gq_t, gk_t, gmat)


def _head_masked(q):
    lane = lax.broadcasted_iota(jnp.int32, q.shape, 1)
    zero = jnp.zeros_like(q)
    return jnp.where(lane < HEAD_DIM, q, zero), jnp.where(lane >= HEAD_DIM, q, zero)


def _scores_t(k_blk, q_masked):
    return lax.dot_general(k_blk, q_masked, (((1,), (1,)), ((), ())), preferred_element_type=F32)


def _fill_vt(v_ref, vt_ref, blk):
    for i in range(vt_ref.shape[0]):
        vt_ref[i] = v_ref[i * blk:(i + 1) * blk, :].astype(F32).T.astype(BF16)


def _merge_heads_t(out_a, out_b):
    row = lax.broadcasted_iota(jnp.int32, out_a.shape, 0)
    return jnp.where(row < HEAD_DIM, out_a, out_b).T


def _split_bf16(x):
    hi = lax.bitcast_convert_type(lax.bitcast_convert_type(x, jnp.uint32) & jnp.uint32(0xFFFF0000), F32)
    return hi.astype(BF16), (x - hi).astype(BF16)


def _sb_kernel(q_ref, k_ref, v_ref, o_ref, vt_ref, acc_ref):
    tq = vt_ref.shape[2]
    tk = tq
    n_tiles = q_ref.shape[0] // tq
    qi = pl.program_id(2)

    @pl.when(qi == 0)
    def _():
        _fill_vt(v_ref, vt_ref, tk)

    q_heads = [_head_masked(q_ref[t * tq:(t + 1) * tq, :]) for t in range(n_tiles)]
    row = lax.broadcasted_iota(jnp.int32, (tk, tq), 0)
    lane = lax.broadcasted_iota(jnp.int32, (tk, tq), 1)
    tri = (lane >= row).astype(BF16)
    strict = row < lane

    def blocks(work, carries, first):
        depth = max(len(kbs) for kbs in work.values())
        chains = [(t, n, h) for n in range(depth) for t in work if n < len(work[t]) for h in range(2)]
        k_blk = lambda kb: k_ref[pl.ds(pl.multiple_of(kb * tk, tk), tk), :]
        z = {(t, n, h): _scores_t(k_blk(work[t][n]), q_heads[t][h]) for t, n, h in chains}
        incl = {}
        carries = {t: list(carries[t]) for t in work}
        pv = {}

        def suffix_stage(c):
            if first and c[1] == 0:
                z[c] = jnp.where(strict, z[c], NEG_INF)
            sp = jnp.maximum(z[c], 0.0) + jnp.log(1.0 + jnp.exp2(jnp.abs(z[c]) * (-LOG2E)))
            hi, lo = _split_bf16(sp)
            incl[c] = (jnp.dot(tri, hi, preferred_element_type=F32) +
                       jnp.dot(tri, lo, preferred_element_type=F32))

        def weight_stage(c):
            t, n, h = c
            log_w = z[c] - incl[c]
            if not (first and n == 0):
                log_w = log_w + carries[t][h]
            out = jnp.dot(vt_ref[work[t][n]], jnp.exp(log_w).astype(BF16), preferred_element_type=F32)
            pv[t, h] = out if (t, h) not in pv else pv[t, h] + out
            carries[t][h] = carries[t][h] - incl[c][0:1, :]

        for c in chains:
            suffix_stage(c)
        for c in chains:
            weight_stage(c)
        for t, h in pv:
            if first:
                acc_ref[t, h] = pv[t, h]
            else:
                acc_ref[t, h] += pv[t, h]
        return tuple(tuple(carries[t]) for t in work)

    zero = jnp.zeros((1, tq), F32)
    zeros = {t: (zero, zero) for t in range(n_tiles)}
    base = qi * n_tiles
    full = {t: [base + t, base + t - 1] for t in range(n_tiles)}
    head = dict(full)
    head[0] = [base]
    carries = lax.cond(qi == 0,
                       lambda: blocks(head, zeros, first=True),
                       lambda: blocks(full, zeros, first=True))

    def cond(state):
        kb, ca, cb = state
        alive = jnp.max(jnp.maximum(ca, cb)) > SB_DEAD
        return jnp.logical_and(kb >= 0, alive)

    least_dead = carries[0][0]
    for t in range(n_tiles):
        for h in range(2):
            least_dead = jnp.maximum(least_dead, carries[t][h])

    @pl.when(jnp.max(least_dead) > SB_DEAD)
    def _():
        for t in range(n_tiles):
            def body(state, t=t):
                kb, ca, cb = state
                ((ca, cb),) = blocks({t: [kb]}, {t: (ca, cb)}, first=False)
                return kb - 1, ca, cb

            lax.while_loop(cond, body, (base + t - 2,) + carries[t])

    for t in range(n_tiles):
        o_ref[t * tq:(t + 1) * tq, :] = _merge_heads_t(acc_ref[t, 0], acc_ref[t, 1]).astype(o_ref.dtype)


def _sb_attention(proj):
    b, s, _ = proj.shape
    tq = min(SB_TQ, s)
    step = min(SB_TILES * tq, s)
    return pl.pallas_call(
        _sb_kernel,
        grid=(b, PAIRS, s // step),
        in_specs=[pl.BlockSpec((None, step, LANES), lambda i, p, j: (i, j, p)),
                  pl.BlockSpec((None, s, LANES), lambda i, p, j: (i, 0, PAIRS + p)),
                  pl.BlockSpec((None, s, LANES), lambda i, p, j: (i, 0, 2 * PAIRS + p))],
        out_specs=pl.BlockSpec((None, step, LANES), lambda i, p, j: (i, j, p)),
        out_shape=jax.ShapeDtypeStruct((b, s, D_GROUP), BF16),
        scratch_shapes=[pltpu.VMEM((s // tq, LANES, tq), BF16),
                        pltpu.VMEM((step // tq, 2, LANES, tq), F32)],
        compiler_params=_cparams(3),
        name="stickbreak_attn",
    )(proj, proj, proj)


def _ca_band_rows(i, half, tq):
    start = half * LANES
    stop = (half + 1) * LANES - CHUNK + BAND
    lo = min(max(start, i * tq), (i + 1) * tq) - i * tq
    hi = max(min(stop, (i + 1) * tq), i * tq) - i * tq
    return lo, max(hi, lo)


def _ca_kernel(q_ref, k_ref, v_ref, bias_ref, o_ref, vt_ref, s_ref):
    tq = vt_ref.shape[3]
    n_tiles = q_ref.shape[0] // tq
    n_win = CA_WIN // tq
    qi = pl.program_id(2)

    @pl.when(qi == 0)
    def _():
        row = lax.broadcasted_iota(jnp.int32, (LANES, tq), 0)
        for i in range(vt_ref.shape[1]):
            vt = v_ref[i * tq:(i + 1) * tq, :].astype(F32).T
            vt_ref[0, i] = jnp.where(row < HEAD_DIM, vt, 1.0).astype(BF16)
            vt_ref[1, i] = jnp.where(row >= HEAD_DIM, vt, 1.0).astype(BF16)

    units = [(t, h) for t in range(n_tiles) for h in range(2)]
    halves = [slice(f * LANES, (f + 1) * LANES) for f in range(tq // LANES)]
    q_heads = [_head_masked(q_ref[t * tq:(t + 1) * tq, :]) for t in range(n_tiles)]
    first_blk = [qi * n_tiles + t - (n_win - 1) for t in range(n_tiles)]
    blks = [[jnp.maximum(first_blk[t] + i, 0) for i in range(n_win)] for t in range(n_tiles)]
    k_blk = lambda bi: k_ref[pl.ds(pl.multiple_of(bi * tq, tq), tq), :]

    def attend(sequence_start):
        col_max = {}
        for t, h in units:
            for i in range(n_win):
                raw = _scores_t(k_blk(blks[t][i]), q_heads[t][h])
                for f, lanes in enumerate(halves):
                    lo, hi = _ca_band_rows(i, f, tq)
                    if hi == lo:
                        continue
                    rows = slice(i * tq + lo, i * tq + hi)
                    sc = raw[lo:hi, lanes] + bias_ref[h, rows, lanes]
                    if sequence_start:
                        sc = jnp.where(first_blk[t] + i >= 0, sc, NEG_INF)
                    s_ref[t, h, rows, lanes] = sc
                    blk_max = sc.max(axis=0, keepdims=True)
                    col_max[t, h, f] = (jnp.maximum(col_max[t, h, f], blk_max)
                                        if (t, h, f) in col_max else blk_max)
        for t in range(n_tiles):
            outs = []
            for h in range(2):
                acc = None
                for i in range(n_win):
                    cols = []
                    for f, lanes in enumerate(halves):
                        lo, hi = _ca_band_rows(i, f, tq)
                        parts = [jnp.zeros((lo, LANES), BF16)] if lo else []
                        if hi > lo:
                            rows = slice(i * tq + lo, i * tq + hi)
                            parts.append(jnp.exp(s_ref[t, h, rows, lanes] - col_max[t, h, f]).astype(BF16))
                        if hi < tq:
                            parts.append(jnp.zeros((tq - hi, LANES), BF16))
                        cols.append(jnp.concatenate(parts, axis=0) if len(parts) > 1 else parts[0])
                    p = jnp.concatenate(cols, axis=1)
                    out = jnp.dot(vt_ref[h, blks[t][i]], p, preferred_element_type=F32)
                    acc = out if acc is None else acc + out
                ones_row = HEAD_DIM if h == 0 else 0
                outs.append(acc * (1.0 / acc[ones_row:ones_row + 1, :]))
            o_ref[t * tq:(t + 1) * tq, :] = _merge_heads_t(outs[0], outs[1]).astype(o_ref.dtype)

    n_start_steps = -(-(n_win - 1) // n_tiles)
    pl.when(qi < n_start_steps)(lambda: attend(True))
    pl.when(qi >= n_start_steps)(lambda: attend(False))


def _ca_attention(proj, bias_tab, layer):
    b, s, _ = proj.shape
    tq = CA_TQ
    step = min(CA_TILES * tq, s)
    base = 3 * PAIRS
    return pl.pallas_call(
        _ca_kernel,
        grid=(b, PAIRS, s // step),
        in_specs=[pl.BlockSpec((None, step, LANES), lambda i, p, j: (i, j, base + p)),
                  pl.BlockSpec((None, s, LANES), lambda i, p, j: (i, 0, base + PAIRS + p)),
                  pl.BlockSpec((None, s, LANES), lambda i, p, j: (i, 0, base + 2 * PAIRS + p)),
                  pl.BlockSpec((2, CA_WIN, tq), lambda i, p, j: (layer * PAIRS + p, 0, 0))],
        out_specs=pl.BlockSpec((None, step, LANES), lambda i, p, j: (i, j, p)),
        out_shape=jax.ShapeDtypeStruct((b, s, D_GROUP), BF16),
        scratch_shapes=[pltpu.VMEM((2, s // tq, LANES, tq), BF16),
                        pltpu.VMEM((step // tq, 2, CA_WIN, tq), F32)],
        compiler_params=_cparams(3),
        name="chunkrel_attn",
    )(proj, proj, proj, bias_tab)


BIAS_ROWS = 128
BIAS_EXT = CA_WIN + CA_TQ


def _ca_bias_kernel(ext_ref, o_ref):
    width = CA_TQ + BIAS_ROWS
    r_loc = lax.broadcasted_iota(jnp.int32, (BIAS_ROWS, CA_TQ), 0)
    col = lax.broadcasted_iota(jnp.int32, (BIAS_ROWS, CA_TQ), 1)
    chunk_start = (col // CHUNK) * CHUNK
    for a in range(CA_WIN // BIAS_ROWS):
        start = CA_WIN - BIAS_ROWS * (a + 1)
        m = jnp.broadcast_to(ext_ref[:, start:start + width], (BIAS_ROWS, width))
        m = pltpu.roll(m, 0, 1, stride=1, stride_axis=0)
        k_in_band = r_loc + (a * BIAS_ROWS) - chunk_start
        in_band = jnp.logical_and(k_in_band >= 0, k_in_band < BAND)
        o_ref[a * BIAS_ROWS:(a + 1) * BIAS_ROWS, :] = jnp.where(
            in_band, m[:, BIAS_ROWS:BIAS_ROWS + CA_TQ], NEG_INF)


def _ca_bias_table(rel_bias):
    depth, n_heads, n_rel = rel_bias.shape
    rb = rel_bias.reshape(depth * n_heads, n_rel).astype(F32)
    n_lo = CA_TQ - REL_CLIP
    n_hi = BIAS_EXT - n_lo - n_rel
    ext = jnp.concatenate([jnp.broadcast_to(rb[:, :1], (rb.shape[0], n_lo)), rb,
                           jnp.broadcast_to(rb[:, -1:], (rb.shape[0], n_hi))], axis=1)
    return pl.pallas_call(
        _ca_bias_kernel,
        grid=(depth * n_heads,),
        in_specs=[pl.BlockSpec((None, 1, BIAS_EXT), lambda i: (i, 0, 0))],
        out_specs=pl.BlockSpec((None, CA_WIN, CA_TQ), lambda i: (i, 0, 0)),
        out_shape=jax.ShapeDtypeStruct((depth * n_heads, CA_WIN, CA_TQ), F32),
        compiler_params=_cparams(1),
        name="ca_bias_table",
    )(ext.reshape(depth * n_heads, 1, BIAS_EXT))


def _mlp_kernel(x_ref, osb_ref, oca_ref, mod_ref, g2_ref, wo_ref, w1_ref, w2_ref, o_ref):
    att = (jnp.dot(osb_ref[...], wo_ref[0:D_GROUP, :], preferred_element_type=F32) +
           jnp.dot(oca_ref[...], wo_ref[D_GROUP:, :], preferred_element_type=F32))
    x1 = x_ref[...] + mod_ref[2:3, :] * att
    h = _modulated_norm(x1, g2_ref[...], mod_ref[3:4, :], mod_ref[4:5, :]).astype(BF16)
    d_ff = w1_ref.shape[1]
    fc = 1024
    acc = jnp.zeros(x1.shape, F32)
    for c in range(d_ff // fc):
        u = jnp.maximum(jnp.dot(h, w1_ref[:, c * fc:(c + 1) * fc], preferred_element_type=F32), 0.0)
        acc = acc + jnp.dot((u * u).astype(BF16), w2_ref[c * fc:(c + 1) * fc, :],
                            preferred_element_type=F32)
    o_ref[...] = x1 + mod_ref[5:6, :] * acc


def _outproj_mlp(x, o_sb, o_ca, mod, layer, g2, wo_bf, w1_bf, w2_bf):
    b, s, d = x.shape
    d_ff = w1_bf.shape[1]
    tm = min(ROW_TILE, s)
    return pl.pallas_call(
        _mlp_kernel,
        grid=(b, s // tm),
        in_specs=[pl.BlockSpec((None, tm, d), lambda i, j: (i, j, 0)),
                  pl.BlockSpec((None, tm, D_GROUP), lambda i, j: (i, j, 0)),
                  pl.BlockSpec((None, tm, D_GROUP), lambda i, j: (i, j, 0)),
                  pl.BlockSpec((None, None, 6, d), lambda i, j: (layer, i, 0, 0)),
                  _resident((1, d), lambda i, j: (0, 0)),
                  _resident((d, d), lambda i, j: (0, 0)),
                  _resident((d, d_ff), lambda i, j: (0, 0)),
                  _resident((d_ff, d), lambda i, j: (0, 0))],
        out_specs=pl.BlockSpec((None, tm, d), lambda i, j: (i, j, 0)),
        out_shape=jax.ShapeDtypeStruct((b, s, d), F32),
        compiler_params=_cparams(2),
        name="outproj_mlp",
    )(x, o_sb, o_ca, mod, g2, wo_bf, w1_bf, w2_bf)


def kernel(x, c, g_norm1, w_in, g_q, g_k, rel_bias, w_o, g_norm2, w1, w2, w_ada, b_ada):
    depth = w_in.shape[0]
    mod = _ada_modulation(c, w_ada, b_ada)
    bias_tab = _ca_bias_table(rel_bias)
    lane_head = jnp.arange(D_GROUP) // HEAD_DIM
    gmat = ((lane_head[:, None] == lane_head[None, :]).astype(F32) * (1.0 / HEAD_DIM)).astype(BF16)
    for l in range(depth):
        gq_t = (jnp.tile(g_q[l], HEADS_PER_GROUP) * QK_SCALE)[None, :]
        gk_t = jnp.tile(g_k[l], HEADS_PER_GROUP)[None, :]
        proj = _inproj(x, mod, l, g_norm1[l][None, :], w_in[l].astype(BF16), gq_t, gk_t, gmat)
        o_sb = _sb_attention(proj)
        o_ca = _ca_attention(proj, bias_tab, l)
        x = _outproj_mlp(x, o_sb, o_ca, mod, l, g_norm2[l][None, :],
                         w_o[l].astype(BF16), w1[l].astype(BF16), w2[l].astype(BF16))
    return x
```

```python
import jax
import jax.numpy as jnp
from jax import lax
from jax.experimental import pallas as pl
from jax.experimental.pallas import tpu as pltpu

F32 = jnp.float32
BF16 = jnp.bfloat16

HEAD_DIM = 64
LANES = 128
HEADS_PER_GROUP = 8
PAIRS = HEADS_PER_GROUP // 2
D_GROUP = HEADS_PER_GROUP * HEAD_DIM
CHUNK = 64
LEFT_CHUNKS = 8
BAND = (LEFT_CHUNKS + 1) * CHUNK
REL_CLIP = 128
EPS = 1e-6
NEG_INF = -1e30
QK_SCALE = HEAD_DIM ** -0.5
LOG2E = 1.4426950408889634

ROW_TILE = 512
SB_TQ = 256
SB_TILES = 4
CA_TQ = 256
CA_TILES = 4
CA_WIN = CA_TQ + LEFT_CHUNKS * CHUNK
SB_DEAD = -88.0
VMEM_LIMIT = 56 * 1024 * 1024


def _cparams(n_axes):
    return pltpu.CompilerParams(dimension_semantics=("arbitrary",) * n_axes,
                                vmem_limit_bytes=VMEM_LIMIT)


def _resident(shape, index_map):
    return pl.BlockSpec(shape, index_map, pipeline_mode=pl.Buffered(1))


def _ada_kernel(c_ref, w_ref, b_ref, o_ref):
    ca = jax.nn.silu(c_ref[...]).astype(BF16)
    o_ref[...] = jnp.dot(ca, w_ref[...].astype(BF16), preferred_element_type=F32) + b_ref[...]


def _ada_modulation(c, w_ada, b_ada):
    depth, d, n = w_ada.shape
    b = c.shape[0]
    rows = 8
    c_pad = jnp.pad(c, ((0, rows - b), (0, 0)))
    tn = d
    out = pl.pallas_call(
        _ada_kernel,
        grid=(depth, n // tn),
        in_specs=[pl.BlockSpec((rows, d), lambda l, j: (0, 0)),
                  pl.BlockSpec((None, d, tn), lambda l, j: (l, 0, j)),
                  pl.BlockSpec((None, 1, tn), lambda l, j: (l, 0, j))],
        out_specs=pl.BlockSpec((None, rows, tn), lambda l, j: (l, 0, j)),
        out_shape=jax.ShapeDtypeStruct((depth, rows, n), F32),
        compiler_params=_cparams(2),
        name="ada_modulation",
    )(c_pad, w_ada, b_ada.reshape(depth, 1, n))
    return out[:, :b].reshape(depth, b, 6, d)


def _modulated_norm(x, g, shift, scale):
    ms = jnp.mean(x * x, axis=-1, keepdims=True)
    return (x * lax.rsqrt(ms + EPS) * g) * (1.0 + scale) + shift


def _inproj_kernel(x_ref, mod_ref, g1_ref, w_ref, gq_ref, gk_ref, gmat_ref, o_ref):
    h = _modulated_norm(x_ref[...], g1_ref[...], mod_ref[0:1, :], mod_ref[1:2, :]).astype(BF16)
    for c in range(6):
        cols = slice(c * D_GROUP, (c + 1) * D_GROUP)
        y = jnp.dot(h, w_ref[:, cols], preferred_element_type=F32)
        if c == 0:
            y = y * QK_SCALE
        elif c in (3, 4):
            msq = jnp.dot((y * y).astype(BF16), gmat_ref[...], preferred_element_type=F32)
            y = y * lax.rsqrt(msq + EPS) * (gq_ref[...] if c == 3 else gk_ref[...])
        o_ref[:, cols] = y.astype(BF16)


def _inproj(x, mod, layer, g1, w_in_bf, gq_t, gk_t, gmat):
    b, s, d = x.shape
    n = w_in_bf.shape[1]
    tm = min(ROW_TILE, s)
    return pl.pallas_call(
        _inproj_kernel,
        grid=(b, s // tm),
        in_specs=[pl.BlockSpec((None, tm, d), lambda i, j: (i, j, 0)),
                  pl.BlockSpec((None, None, 6, d), lambda i, j: (layer, i, 0, 0)),
                  _resident((1, d), lambda i, j: (0, 0)),
                  _resident((d, n), lambda i, j: (0, 0)),
                  _resident((1, D_GROUP), lambda i, j: (0, 0)),
                  _resident((1, D_GROUP), lambda i, j: (0, 0)),
                  _resident((D_GROUP, D_GROUP), lambda i, j: (0, 0))],
        out_specs=pl.BlockSpec((None, tm, n), lambda i, j: (i, j, 0)),
        out_shape=jax.ShapeDtypeStruct((b, s, n), BF16),
        compiler_params=_cparams(2),
        name="norm1_inproj",
    )(x, mod, g1, w_in_bf, gq_t, gk_t, gmat)


def _head_masked(q):
    lane = lax.broadcasted_iota(jnp.int32, q.shape, 1)
    zero = jnp.zeros_like(q)
    return jnp.where(lane < HEAD_DIM, q, zero), jnp.where(lane >= HEAD_DIM, q, zero)


def _scores_t(k_blk, q_masked):
    return lax.dot_general(k_blk, q_masked, (((1,), (1,)), ((), ())), preferred_element_type=F32)


def _fill_vt(v_ref, vt_ref, blk):
    for i in range(vt_ref.shape[0]):
        vt_ref[i] = v_ref[i * blk:(i + 1) * blk, :].astype(F32).T.astype(BF16)


def _merge_heads_t(out_a, out_b):
    row = lax.broadcasted_iota(jnp.int32, out_a.shape, 0)
    return jnp.where(row < HEAD_DIM, out_a, out_b).T


def _split_bf16(x):
    hi = lax.bitcast_convert_type(lax.bitcast_convert_type(x, jnp.uint32) & jnp.uint32(0xFFFF0000), F32)
    return hi.astype(BF16), (x - hi).astype(BF16)


def _sb_kernel(q_ref, k_ref, v_ref, o_ref, vt_ref, acc_ref):
    tq = vt_ref.shape[2]
    tk = tq
    n_tiles = q_ref.shape[0] // tq
    qi = pl.program_id(2)

    @pl.when(qi == 0)
    def _():
        _fill_vt(v_ref, vt_ref, tk)

    q_heads = [_head_masked(q_ref[t * tq:(t + 1) * tq, :]) for t in range(n_tiles)]
    row = lax.broadcasted_iota(jnp.int32, (tk, tq), 0)
    lane = lax.broadcasted_iota(jnp.int32, (tk, tq), 1)
    tri = (lane >= row).astype(BF16)
    strict = row < lane

    def blocks(work, carries, first):
        depth = max(len(kbs) for kbs in work.values())
        chains = [(t, n, h) for n in range(depth) for t in work if n < len(work[t]) for h in range(2)]
        k_blk = lambda kb: k_ref[pl.ds(pl.multiple_of(kb * tk, tk), tk), :]
        z = {(t, n, h): _scores_t(k_blk(work[t][n]), q_heads[t][h]) for t, n, h in chains}
        incl = {}
        carries = {t: list(carries[t]) for t in work}
        pv = {}

        def suffix_stage(c):
            if first and c[1] == 0:
                z[c] = jnp.where(strict, z[c], NEG_INF)
            sp = jnp.maximum(z[c], 0.0) + jnp.log(1.0 + jnp.exp2(jnp.abs(z[c]) * (-LOG2E)))
            hi, lo = _split_bf16(sp)
            incl[c] = (jnp.dot(tri, hi, preferred_element_type=F32) +
                       jnp.dot(tri, lo, preferred_element_type=F32))

        def weight_stage(c):
            t, n, h = c
            log_w = z[c] - incl[c]
            if not (first and n == 0):
                log_w = log_w + carries[t][h]
            out = jnp.dot(vt_ref[work[t][n]], jnp.exp(log_w).astype(BF16), preferred_element_type=F32)
            pv[t, h] = out if (t, h) not in pv else pv[t, h] + out
            carries[t][h] = carries[t][h] - incl[c][0:1, :]

        for c in chains:
            suffix_stage(c)
        for c in chains:
            weight_stage(c)
        for t, h in pv:
            if first:
                acc_ref[t, h] = pv[t, h]
            else:
                acc_ref[t, h] += pv[t, h]
        return tuple(tuple(carries[t]) for t in work)

    zero = jnp.zeros((1, tq), F32)
    zeros = {t: (zero, zero) for t in range(n_tiles)}
    base = qi * n_tiles
    full = {t: [base + t, base + t - 1] for t in range(n_tiles)}
    head = dict(full)
    head[0] = [base]
    carries = lax.cond(qi == 0,
                       lambda: blocks(head, zeros, first=True),
                       lambda: blocks(full, zeros, first=True))

    def cond(state):
        kb, ca, cb = state
        alive = jnp.max(jnp.maximum(ca, cb)) > SB_DEAD
        return jnp.logical_and(kb >= 0, alive)

    least_dead = carries[0][0]
    for t in range(n_tiles):
        for h in range(2):
            least_dead = jnp.maximum(least_dead, carries[t][h])

    @pl.when(jnp.max(least_dead) > SB_DEAD)
    def _():
        for t in range(n_tiles):
            def body(state, t=t):
                kb, ca, cb = state
                ((ca, cb),) = blocks({t: [kb]}, {t: (ca, cb)}, first=False)
                return kb - 1, ca, cb

            lax.while_loop(cond, body, (base + t - 2,) + carries[t])

    for t in range(n_tiles):
        o_ref[t * tq:(t + 1) * tq, :] = _merge_heads_t(acc_ref[t, 0], acc_ref[t, 1]).astype(o_ref.dtype)


def _sb_attention(proj):
    b, s, _ = proj.shape
    tq = min(SB_TQ, s)
    step = min(SB_TILES * tq, s)
    return pl.pallas_call(
        _sb_kernel,
        grid=(b, PAIRS, s // step),
        in_specs=[pl.BlockSpec((None, step, LANES), lambda i, p, j: (i, j, p)),
                  pl.BlockSpec((None, s, LANES), lambda i, p, j: (i, 0, PAIRS + p)),
                  pl.BlockSpec((None, s, LANES), lambda i, p, j: (i, 0, 2 * PAIRS + p))],
        out_specs=pl.BlockSpec((None, step, LANES), lambda i, p, j: (i, j, p)),
        out_shape=jax.ShapeDtypeStruct((b, s, D_GROUP), BF16),
        scratch_shapes=[pltpu.VMEM((s // tq, LANES, tq), BF16),
                        pltpu.VMEM((step // tq, 2, LANES, tq), F32)],
        compiler_params=_cparams(3),
        name="stickbreak_attn",
    )(proj, proj, proj)


def _ca_band_rows(i, half, tq):
    start = half * LANES
    stop = (half + 1) * LANES - CHUNK + BAND
    lo = min(max(start, i * tq), (i + 1) * tq) - i * tq
    hi = max(min(stop, (i + 1) * tq), i * tq) - i * tq
    return lo, max(hi, lo)


def _ca_kernel(q_ref, k_ref, v_ref, bias_ref, o_ref, vt_ref, s_ref):
    tq = vt_ref.shape[3]
    n_tiles = q_ref.shape[0] // tq
    n_win = CA_WIN // tq
    qi = pl.program_id(2)

    @pl.when(qi == 0)
    def _():
        row = lax.broadcasted_iota(jnp.int32, (LANES, tq), 0)
        for i in range(vt_ref.shape[1]):
            vt = v_ref[i * tq:(i + 1) * tq, :].astype(F32).T
            vt_ref[0, i] = jnp.where(row < HEAD_DIM, vt, 1.0).astype(BF16)
            vt_ref[1, i] = jnp.where(row >= HEAD_DIM, vt, 1.0).astype(BF16)

    units = [(t, h) for t in range(n_tiles) for h in range(2)]
    halves = [slice(f * LANES, (f + 1) * LANES) for f in range(tq // LANES)]
    q_heads = [_head_masked(q_ref[t * tq:(t + 1) * tq, :]) for t in range(n_tiles)]
    first_blk = [qi * n_tiles + t - (n_win - 1) for t in range(n_tiles)]
    blks = [[jnp.maximum(first_blk[t] + i, 0) for i in range(n_win)] for t in range(n_tiles)]
    k_blk = lambda bi: k_ref[pl.ds(pl.multiple_of(bi * tq, tq), tq), :]

    def attend(sequence_start):
        col_max = {}
        for t, h in units:
            for i in range(n_win):
                raw = _scores_t(k_blk(blks[t][i]), q_heads[t][h])
                for f, lanes in enumerate(halves):
                    lo, hi = _ca_band_rows(i, f, tq)
                    if hi == lo:
                        continue
                    rows = slice(i * tq + lo, i * tq + hi)
                    sc = raw[lo:hi, lanes] + bias_ref[h, rows, lanes]
                    if sequence_start:
                        sc = jnp.where(first_blk[t] + i >= 0, sc, NEG_INF)
                    s_ref[t, h, rows, lanes] = sc
                    blk_max = sc.max(axis=0, keepdims=True)
                    col_max[t, h, f] = (jnp.maximum(col_max[t, h, f], blk_max)
                                        if (t, h, f) in col_max else blk_max)
        for t in range(n_tiles):
            outs = []
            for h in range(2):
                acc = None
                for i in range(n_win):
                    cols = []
                    for f, lanes in enumerate(halves):
                        lo, hi = _ca_band_rows(i, f, tq)
                        parts = [jnp.zeros((lo, LANES), BF16)] if lo else []
                        if hi > lo:
                            rows = slice(i * tq + lo, i * tq + hi)
                            parts.append(jnp.exp(s_ref[t, h, rows, lanes] - col_max[t, h, f]).astype(BF16))
                        if hi < tq:
                            parts.append(jnp.zeros((tq - hi, LANES), BF16))
                        cols.append(jnp.concatenate(parts, axis=0) if len(parts) > 1 else parts[0])
                    p = jnp.concatenate(cols, axis=1)
                    out = jnp.dot(vt_ref[h, blks[t][i]], p, preferred_element_type=F32)
                    acc = out if acc is None else acc + out
                ones_row = HEAD_DIM if h == 0 else 0
                outs.append(acc * (1.0 / acc[ones_row:ones_row + 1, :]))
            o_ref[t * tq:(t + 1) * tq, :] = _merge_heads_t(outs[0], outs[1]).astype(o_ref.dtype)

    n_start_steps = -(-(n_win - 1) // n_tiles)
    pl.when(qi < n_start_steps)(lambda: attend(True))
    pl.when(qi >= n_start_steps)(lambda: attend(False))


def _ca_attention(proj, bias_tab, layer):
    b, s, _ = proj.shape
    tq = CA_TQ
    step = min(CA_TILES * tq, s)
    base = 3 * PAIRS
    return pl.pallas_call(
        _ca_kernel,
        grid=(b, PAIRS, s // step),
        in_specs=[pl.BlockSpec((None, step, LANES), lambda i, p, j: (i, j, base + p)),
                  pl.BlockSpec((None, s, LANES), lambda i, p, j: (i, 0, base + PAIRS + p)),
                  pl.BlockSpec((None, s, LANES), lambda i, p, j: (i, 0, base + 2 * PAIRS + p)),
                  pl.BlockSpec((2, CA_WIN, tq), lambda i, p, j: (layer * PAIRS + p, 0, 0))],
        out_specs=pl.BlockSpec((None, step, LANES), lambda i, p, j: (i, j, p)),
        out_shape=jax.ShapeDtypeStruct((b, s, D_GROUP), BF16),
        scratch_shapes=[pltpu.VMEM((2, s // tq, LANES, tq), BF16),
                        pltpu.VMEM((step // tq, 2, CA_WIN, tq), F32)],
        compiler_params=_cparams(3),
        name="chunkrel_attn",
    )(proj, proj, proj, bias_tab)


BIAS_ROWS = 128
BIAS_EXT = CA_WIN + CA_TQ


def _ca_bias_kernel(ext_ref, o_ref):
    width = CA_TQ + BIAS_ROWS
    r_loc = lax.broadcasted_iota(jnp.int32, (BIAS_ROWS, CA_TQ), 0)
    col = lax.broadcasted_iota(jnp.int32, (BIAS_ROWS, CA_TQ), 1)
    chunk_start = (col // CHUNK) * CHUNK
    for a in range(CA_WIN // BIAS_ROWS):
        start = CA_WIN - BIAS_ROWS * (a + 1)
        m = jnp.broadcast_to(ext_ref[:, start:start + width], (BIAS_ROWS, width))
        m = pltpu.roll(m, 0, 1, stride=1, stride_axis=0)
        k_in_band = r_loc + (a * BIAS_ROWS) - chunk_start
        in_band = jnp.logical_and(k_in_band >= 0, k_in_band < BAND)
        o_ref[a * BIAS_ROWS:(a + 1) * BIAS_ROWS, :] = jnp.where(
            in_band, m[:, BIAS_ROWS:BIAS_ROWS + CA_TQ], NEG_INF)


def _ca_bias_table(rel_bias):
    depth, n_heads, n_rel = rel_bias.shape
    rb = rel_bias.reshape(depth * n_heads, n_rel).astype(F32)
    n_lo = CA_TQ - REL_CLIP
    n_hi = BIAS_EXT - n_lo - n_rel
    ext = jnp.concatenate([jnp.broadcast_to(rb[:, :1], (rb.shape[0], n_lo)), rb,
                           jnp.broadcast_to(rb[:, -1:], (rb.shape[0], n_hi))], axis=1)
    return pl.pallas_call(
        _ca_bias_kernel,
        grid=(depth * n_heads,),
        in_specs=[pl.BlockSpec((None, 1, BIAS_EXT), lambda i: (i, 0, 0))],
        out_specs=pl.BlockSpec((None, CA_WIN, CA_TQ), lambda i: (i, 0, 0)),
        out_shape=jax.ShapeDtypeStruct((depth * n_heads, CA_WIN, CA_TQ), F32),
        compiler_params=_cparams(1),
        name="ca_bias_table",
    )(ext.reshape(depth * n_heads, 1, BIAS_EXT))


def _mlp_kernel(x_ref, osb_ref, oca_ref, mod_ref, g2_ref, wo_ref, w1_ref, w2_ref, o_ref):
    att = (jnp.dot(osb_ref[...], wo_ref[0:D_GROUP, :], preferred_element_type=F32) +
           jnp.dot(oca_ref[...], wo_ref[D_GROUP:, :], preferred_element_type=F32))
    x1 = x_ref[...] + mod_ref[2:3, :] * att
    h = _modulated_norm(x1, g2_ref[...], mod_ref[3:4, :], mod_ref[4:5, :]).astype(BF16)
    d_ff = w1_ref.shape[1]
    fc = 1024
    acc = jnp.zeros(x1.shape, F32)
    for c in range(d_ff // fc):
        u = jnp.maximum(jnp.dot(h, w1_ref[:, c * fc:(c + 1) * fc], preferred_element_type=F32), 0.0)
        acc = acc + jnp.dot((u * u).astype(BF16), w2_ref[c * fc:(c + 1) * fc, :],
                            preferred_element_type=F32)
    o_ref[...] = x1 + mod_ref[5:6, :] * acc


def _outproj_mlp(x, o_sb, o_ca, mod, layer, g2, wo_bf, w1_bf, w2_bf):
    b, s, d = x.shape
    d_ff = w1_bf.shape[1]
    tm = min(ROW_TILE, s)
    return pl.pallas_call(
        _mlp_kernel,
        grid=(b, s // tm),
        in_specs=[pl.BlockSpec((None, tm, d), lambda i, j: (i, j, 0)),
                  pl.BlockSpec((None, tm, D_GROUP), lambda i, j: (i, j, 0)),
                  pl.BlockSpec((None, tm, D_GROUP), lambda i, j: (i, j, 0)),
                  pl.BlockSpec((None, None, 6, d), lambda i, j: (layer, i, 0, 0)),
                  _resident((1, d), lambda i, j: (0, 0)),
                  _resident((d, d), lambda i, j: (0, 0)),
                  _resident((d, d_ff), lambda i, j: (0, 0)),
                  _resident((d_ff, d), lambda i, j: (0, 0))],
        out_specs=pl.BlockSpec((None, tm, d), lambda i, j: (i, j, 0)),
        out_shape=jax.ShapeDtypeStruct((b, s, d), F32),
        compiler_params=_cparams(2),
        name="outproj_mlp",
    )(x, o_sb, o_ca, mod, g2, wo_bf, w1_bf, w2_bf)


def kernel(x, c, g_norm1, w_in, g_q, g_k, rel_bias, w_o, g_norm2, w1, w2, w_ada, b_ada):
    depth = w_in.shape[0]
    mod = _ada_modulation(c, w_ada, b_ada)
    bias_tab = _ca_bias_table(rel_bias)
    lane_head = jnp.arange(D_GROUP) // HEAD_DIM
    gmat = ((lane_head[:, None] == lane_head[None, :]).astype(F32) * (1.0 / HEAD_DIM)).astype(BF16)
    for l in range(depth):
        gq_t = (jnp.tile(g_q[l], HEADS_PER_GROUP) * QK_SCALE)[None, :]
        gk_t = jnp.tile(g_k[l], HEADS_PER_GROUP)[None, :]
        proj = _inproj(x, mod, l, g_norm1[l][None, :], w_in[l].astype(BF16), gq_t, gk_t, gmat)
        o_sb = _sb_attention(proj)
        o_ca = _ca_attention(proj, bias_tab, l)
        x = _outproj_mlp(x, o_sb, o_ca, mod, l, g_norm2[l][None, :],
                         w_o[l].astype(BF16), w1[l].astype(BF16), w2[l].astype(BF16))
    return x
```

```python
import jax
import jax.numpy as jnp
from jax import lax
from jax.experimental import pallas as pl
from jax.experimental.pallas import tpu as pltpu

F32 = jnp.float32
BF16 = jnp.bfloat16

HEAD_DIM = 64
LANES = 128
HEADS_PER_GROUP = 8
PAIRS = HEADS_PER_GROUP // 2
D_GROUP = HEADS_PER_GROUP * HEAD_DIM
CHUNK = 64
LEFT_CHUNKS = 8
BAND = (LEFT_CHUNKS + 1) * CHUNK
REL_CLIP = 128
EPS = 1e-6
NEG_INF = -1e30
QK_SCALE = HEAD_DIM ** -0.5
LOG2E = 1.4426950408889634
SOFTPLUS_LINEAR = 64.0

ROW_TILE = 512
SB_TQ = 256
SB_TILES = 4
CA_TQ = 256
CA_TILES = 8
CA_WIN = CA_TQ + LEFT_CHUNKS * CHUNK
SB_DEAD = -88.0
VMEM_LIMIT = 56 * 1024 * 1024


def _cparams(n_axes):
    return pltpu.CompilerParams(dimension_semantics=("arbitrary",) * n_axes,
                                vmem_limit_bytes=VMEM_LIMIT)


def _resident(shape, index_map):
    return pl.BlockSpec(shape, index_map, pipeline_mode=pl.Buffered(1))


def _ada_kernel(c_ref, w_ref, b_ref, o_ref):
    ca = jax.nn.silu(c_ref[...]).astype(BF16)
    o_ref[...] = jnp.dot(ca, w_ref[...].astype(BF16), preferred_element_type=F32) + b_ref[...]


def _ada_modulation(c, w_ada, b_ada):
    depth, d, n = w_ada.shape
    b = c.shape[0]
    rows = 8
    c_pad = jnp.pad(c, ((0, rows - b), (0, 0)))
    tn = d
    out = pl.pallas_call(
        _ada_kernel,
        grid=(depth, n // tn),
        in_specs=[pl.BlockSpec((rows, d), lambda l, j: (0, 0)),
                  pl.BlockSpec((None, d, tn), lambda l, j: (l, 0, j)),
                  pl.BlockSpec((None, 1, tn), lambda l, j: (l, 0, j))],
        out_specs=pl.BlockSpec((None, rows, tn), lambda l, j: (l, 0, j)),
        out_shape=jax.ShapeDtypeStruct((depth, rows, n), F32),
        compiler_params=_cparams(2),
        name="ada_modulation",
    )(c_pad, w_ada, b_ada.reshape(depth, 1, n))
    return out[:, :b].reshape(depth, b, 6, d)


def _modulated_norm(x, g, shift, scale):
    ms = jnp.mean(x * x, axis=-1, keepdims=True)
    return (x * lax.rsqrt(ms + EPS) * g) * (1.0 + scale) + shift


def _inproj_kernel(x_ref, mod_ref, g1_ref, w_ref, gq_ref, gk_ref, gmat_ref, o_ref):
    h = _modulated_norm(x_ref[...], g1_ref[...], mod_ref[0:1, :], mod_ref[1:2, :]).astype(BF16)
    for c in range(6):
        cols = slice(c * D_GROUP, (c + 1) * D_GROUP)
        y = jnp.dot(h, w_ref[:, cols], preferred_element_type=F32)
        if c == 0:
            y = y * QK_SCALE
        elif c in (3, 4):
            msq = jnp.dot((y * y).astype(BF16), gmat_ref[...], preferred_element_type=F32)
            y = y * lax.rsqrt(msq + EPS) * (gq_ref[...] if c == 3 else gk_ref[...])
        o_ref[:, cols] = y.astype(BF16)


def _inproj(x, mod, layer, g1, w_in_bf, gq_t, gk_t, gmat):
    b, s, d = x.shape
    n = w_in_bf.shape[1]
    tm = min(ROW_TILE, s)
    return pl.pallas_call(
        _inproj_kernel,
        grid=(b, s // tm),
        in_specs=[pl.BlockSpec((None, tm, d), lambda i, j: (i, j, 0)),
                  pl.BlockSpec((None, None, 6, d), lambda i, j: (layer, i, 0, 0)),
                  _resident((1, d), lambda i, j: (0, 0)),
                  _resident((d, n), lambda i, j: (0, 0)),
                  _resident((1, D_GROUP), lambda i, j: (0, 0)),
                  _resident((1, D_GROUP), lambda i, j: (0, 0)),
                  _resident((D_GROUP, D_GROUP), lambda i, j: (0, 0))],
        out_specs=pl.BlockSpec((None, tm, n), lambda i, j: (i, j, 0)),
        out_shape=jax.ShapeDtypeStruct((b, s, n), BF16),
        compiler_params=_cparams(2),
        name="norm1_inproj",
    )(x, mod, g1, w_in_bf, gq_t, gk_t, gmat)


def _head_masked(q):
    lane = lax.broadcasted_iota(jnp.int32, q.shape, 1)
    zero = jnp.zeros_like(q)
    return jnp.where(lane < HEAD_DIM, q, zero), jnp.where(lane >= HEAD_DIM, q, zero)


def _scores_t(k_blk, q_masked):
    return lax.dot_general(k_blk, q_masked, (((1,), (1,)), ((), ())), preferred_element_type=F32)


def _fill_vt(v_ref, vt_ref, blk):
    for i in range(vt_ref.shape[0]):
        vt_ref[i] = v_ref[i * blk:(i + 1) * blk, :].astype(F32).T.astype(BF16)


def _merge_heads_t(out_a, out_b):
    row = lax.broadcasted_iota(jnp.int32, out_a.shape, 0)
    return jnp.where(row < HEAD_DIM, out_a, out_b).T


def _split_bf16(x):
    hi = lax.bitcast_convert_type(lax.bitcast_convert_type(x, jnp.uint32) & jnp.uint32(0xFFFF0000), F32)
    return hi.astype(BF16), (x - hi).astype(BF16)


def _sb_kernel(q_ref, k_ref, v_ref, o_ref, vt_ref, acc_ref):
    tq = vt_ref.shape[2]
    tk = tq
    n_tiles = q_ref.shape[0] // tq
    qi = pl.program_id(2)

    @pl.when(qi == 0)
    def _():
        _fill_vt(v_ref, vt_ref, tk)

    q_heads = [_head_masked(q_ref[t * tq:(t + 1) * tq, :]) for t in range(n_tiles)]
    row = lax.broadcasted_iota(jnp.int32, (tk, tq), 0)
    lane = lax.broadcasted_iota(jnp.int32, (tk, tq), 1)
    tri = (lane >= row).astype(BF16)
    strict = row < lane

    def blocks(work, carries, first):
        depth = max(len(kbs) for kbs in work.values())
        chains = [(t, n, h) for n in range(depth) for t in work if n < len(work[t]) for h in range(2)]
        k_blk = lambda kb: k_ref[pl.ds(pl.multiple_of(kb * tk, tk), tk), :]
        z = {(t, n, h): _scores_t(k_blk(work[t][n]), q_heads[t][h]) for t, n, h in chains}
        incl = {}
        carries = {t: list(carries[t]) for t in work}
        pv = {}

        def suffix_stage(c):
            if first and c[1] == 0:
                z[c] = jnp.where(strict, z[c], NEG_INF)
            sp = jnp.maximum(z[c], jnp.log(1.0 + jnp.exp2(jnp.minimum(z[c], SOFTPLUS_LINEAR) * LOG2E)))
            hi, lo = _split_bf16(sp)
            incl[c] = (jnp.dot(tri, hi, preferred_element_type=F32) +
                       jnp.dot(tri, lo, preferred_element_type=F32))

        def weight_stage(c):
            t, n, h = c
            log_w = z[c] - incl[c]
            if not (first and n == 0):
                log_w = log_w + carries[t][h]
            out = jnp.dot(vt_ref[work[t][n]], jnp.exp(log_w).astype(BF16), preferred_element_type=F32)
            pv[t, h] = out if (t, h) not in pv else pv[t, h] + out
            carries[t][h] = carries[t][h] - incl[c][0:1, :]

        for c in chains:
            suffix_stage(c)
        for c in chains:
            weight_stage(c)
        for t, h in pv:
            if first:
                acc_ref[t, h] = pv[t, h]
            else:
                acc_ref[t, h] += pv[t, h]
        return tuple(tuple(carries[t]) for t in work)

    zero = jnp.zeros((1, tq), F32)
    zeros = {t: (zero, zero) for t in range(n_tiles)}
    base = qi * n_tiles
    full = {t: [base + t, base + t - 1] for t in range(n_tiles)}
    head = dict(full)
    head[0] = [base]
    carries = lax.cond(qi == 0,
                       lambda: blocks(head, zeros, first=True),
                       lambda: blocks(full, zeros, first=True))

    def cond(state):
        kb, ca, cb = state
        alive = jnp.max(jnp.maximum(ca, cb)) > SB_DEAD
        return jnp.logical_and(kb >= 0, alive)

    least_dead = carries[0][0]
    for t in range(n_tiles):
        for h in range(2):
            least_dead = jnp.maximum(least_dead, carries[t][h])

    @pl.when(jnp.max(least_dead) > SB_DEAD)
    def _():
        for t in range(n_tiles):
            def body(state, t=t):
                kb, ca, cb = state
                ((ca, cb),) = blocks({t: [kb]}, {t: (ca, cb)}, first=False)
                return kb - 1, ca, cb

            lax.while_loop(cond, body, (base + t - 2,) + carries[t])

    for t in range(n_tiles):
        o_ref[t * tq:(t + 1) * tq, :] = _merge_heads_t(acc_ref[t, 0], acc_ref[t, 1]).astype(o_ref.dtype)


def _sb_attention(proj):
    b, s, _ = proj.shape
    tq = min(SB_TQ, s)
    step = min(SB_TILES * tq, s)
    return pl.pallas_call(
        _sb_kernel,
        grid=(b, PAIRS, s // step),
        in_specs=[pl.BlockSpec((None, step, LANES), lambda i, p, j: (i, j, p)),
                  pl.BlockSpec((None, s, LANES), lambda i, p, j: (i, 0, PAIRS + p)),
                  pl.BlockSpec((None, s, LANES), lambda i, p, j: (i, 0, 2 * PAIRS + p))],
        out_specs=pl.BlockSpec((None, step, LANES), lambda i, p, j: (i, j, p)),
        out_shape=jax.ShapeDtypeStruct((b, s, D_GROUP), BF16),
        scratch_shapes=[pltpu.VMEM((s // tq, LANES, tq), BF16),
                        pltpu.VMEM((step // tq, 2, LANES, tq), F32)],
        compiler_params=_cparams(3),
        name="stickbreak_attn",
    )(proj, proj, proj)


def _ca_band_rows(i, half, tq):
    start = half * LANES
    stop = (half + 1) * LANES - CHUNK + BAND
    lo = min(max(start, i * tq), (i + 1) * tq) - i * tq
    hi = max(min(stop, (i + 1) * tq), i * tq) - i * tq
    return lo, max(hi, lo)


def _ca_kernel(q_ref, k_ref, v_ref, bias_ref, o_ref, vt_ref, s_ref):
    tq = vt_ref.shape[3]
    n_tiles = q_ref.shape[0] // tq
    n_win = CA_WIN // tq
    qi = pl.program_id(2)

    @pl.when(qi == 0)
    def _():
        row = lax.broadcasted_iota(jnp.int32, (LANES, tq), 0)
        for i in range(vt_ref.shape[1]):
            vt = v_ref[i * tq:(i + 1) * tq, :].astype(F32).T
            vt_ref[0, i] = jnp.where(row < HEAD_DIM, vt, 1.0).astype(BF16)
            vt_ref[1, i] = jnp.where(row >= HEAD_DIM, vt, 1.0).astype(BF16)

    units = [(t, h) for t in range(n_tiles) for h in range(2)]
    halves = [slice(f * LANES, (f + 1) * LANES) for f in range(tq // LANES)]
    q_heads = [_head_masked(q_ref[t * tq:(t + 1) * tq, :]) for t in range(n_tiles)]
    first_blk = [qi * n_tiles + t - (n_win - 1) for t in range(n_tiles)]
    blks = [[jnp.maximum(first_blk[t] + i, 0) for i in range(n_win)] for t in range(n_tiles)]
    k_blk = lambda bi: k_ref[pl.ds(pl.multiple_of(bi * tq, tq), tq), :]

    def attend(sequence_start):
        col_max = {}
        for t, h in units:
            for i in range(n_win):
                raw = _scores_t(k_blk(blks[t][i]), q_heads[t][h])
                for f, lanes in enumerate(halves):
                    lo, hi = _ca_band_rows(i, f, tq)
                    if hi == lo:
                        continue
                    rows = slice(i * tq + lo, i * tq + hi)
                    sc = raw[lo:hi, lanes] + bias_ref[h, rows, lanes]
                    if sequence_start and t + i < n_win - 1:
                        sc = jnp.where(first_blk[t] + i >= 0, sc, NEG_INF)
                    s_ref[t, h, rows, lanes] = sc
                    blk_max = sc.max(axis=0, keepdims=True)
                    col_max[t, h, f] = (jnp.maximum(col_max[t, h, f], blk_max)
                                        if (t, h, f) in col_max else blk_max)
        for t in range(n_tiles):
            outs = []
            for h in range(2):
                acc = None
                for i in range(n_win):
                    cols = []
                    for f, lanes in enumerate(halves):
                        lo, hi = _ca_band_rows(i, f, tq)
                        parts = [jnp.zeros((lo, LANES), BF16)] if lo else []
                        if hi > lo:
                            rows = slice(i * tq + lo, i * tq + hi)
                            parts.append(jnp.exp(s_ref[t, h, rows, lanes] - col_max[t, h, f]).astype(BF16))
                        if hi < tq:
                            parts.append(jnp.zeros((tq - hi, LANES), BF16))
                        cols.append(jnp.concatenate(parts, axis=0) if len(parts) > 1 else parts[0])
                    p = jnp.concatenate(cols, axis=1)
                    out = jnp.dot(vt_ref[h, blks[t][i]], p, preferred_element_type=F32)
                    acc = out if acc is None else acc + out
                ones_row = HEAD_DIM if h == 0 else 0
                outs.append(acc * (1.0 / acc[ones_row:ones_row + 1, :]))
            o_ref[t * tq:(t + 1) * tq, :] = _merge_heads_t(outs[0], outs[1]).astype(o_ref.dtype)

    n_start_steps = -(-(n_win - 1) // n_tiles)
    pl.when(qi < n_start_steps)(lambda: attend(True))
    pl.when(qi >= n_start_steps)(lambda: attend(False))


def _ca_attention(proj, bias_tab, layer):
    b, s, _ = proj.shape
    tq = CA_TQ
    step = min(CA_TILES * tq, s)
    base = 3 * PAIRS
    return pl.pallas_call(
        _ca_kernel,
        grid=(b, PAIRS, s // step),
        in_specs=[pl.BlockSpec((None, step, LANES), lambda i, p, j: (i, j, base + p)),
                  pl.BlockSpec((None, s, LANES), lambda i, p, j: (i, 0, base + PAIRS + p)),
                  pl.BlockSpec((None, s, LANES), lambda i, p, j: (i, 0, base + 2 * PAIRS + p)),
                  pl.BlockSpec((2, CA_WIN, tq), lambda i, p, j: (layer * PAIRS + p, 0, 0))],
        out_specs=pl.BlockSpec((None, step, LANES), lambda i, p, j: (i, j, p)),
        out_shape=jax.ShapeDtypeStruct((b, s, D_GROUP), BF16),
        scratch_shapes=[pltpu.VMEM((2, s // tq, LANES, tq), BF16),
                        pltpu.VMEM((step // tq, 2, CA_WIN, tq), F32)],
        compiler_params=_cparams(3),
        name="chunkrel_attn",
    )(proj, proj, proj, bias_tab)


BIAS_ROWS = 128
BIAS_EXT = CA_WIN + CA_TQ


def _ca_bias_kernel(ext_ref, o_ref):
    width = CA_TQ + BIAS_ROWS
    r_loc = lax.broadcasted_iota(jnp.int32, (BIAS_ROWS, CA_TQ), 0)
    col = lax.broadcasted_iota(jnp.int32, (BIAS_ROWS, CA_TQ), 1)
    chunk_start = (col // CHUNK) * CHUNK
    for a in range(CA_WIN // BIAS_ROWS):
        start = CA_WIN - BIAS_ROWS * (a + 1)
        m = jnp.broadcast_to(ext_ref[:, start:start + width], (BIAS_ROWS, width))
        m = pltpu.roll(m, 0, 1, stride=1, stride_axis=0)
        k_in_band = r_loc + (a * BIAS_ROWS) - chunk_start
        in_band = jnp.logical_and(k_in_band >= 0, k_in_band < BAND)
        o_ref[a * BIAS_ROWS:(a + 1) * BIAS_ROWS, :] = jnp.where(
            in_band, m[:, BIAS_ROWS:BIAS_ROWS + CA_TQ], NEG_INF)


def _ca_bias_table(rel_bias):
    depth, n_heads, n_rel = rel_bias.shape
    rb = rel_bias.reshape(depth * n_heads, n_rel).astype(F32)
    n_lo = CA_TQ - REL_CLIP
    n_hi = BIAS_EXT - n_lo - n_rel
    ext = jnp.concatenate([jnp.broadcast_to(rb[:, :1], (rb.shape[0], n_lo)), rb,
                           jnp.broadcast_to(rb[:, -1:], (rb.shape[0], n_hi))], axis=1)
    return pl.pallas_call(
        _ca_bias_kernel,
        grid=(depth * n_heads,),
        in_specs=[pl.BlockSpec((None, 1, BIAS_EXT), lambda i: (i, 0, 0))],
        out_specs=pl.BlockSpec((None, CA_WIN, CA_TQ), lambda i: (i, 0, 0)),
        out_shape=jax.ShapeDtypeStruct((depth * n_heads, CA_WIN, CA_TQ), F32),
        compiler_params=_cparams(1),
        name="ca_bias_table",
    )(ext.reshape(depth * n_heads, 1, BIAS_EXT))


def _mlp_kernel(x_ref, osb_ref, oca_ref, mod_ref, g2_ref, wo_ref, w1_ref, w2_ref, o_ref):
    att = (jnp.dot(osb_ref[...], wo_ref[0:D_GROUP, :], preferred_element_type=F32) +
           jnp.dot(oca_ref[...], wo_ref[D_GROUP:, :], preferred_element_type=F32))
    x1 = x_ref[...] + mod_ref[2:3, :] * att
    h = _modulated_norm(x1, g2_ref[...], mod_ref[3:4, :], mod_ref[4:5, :]).astype(BF16)
    d_ff = w1_ref.shape[1]
    fc = 1024
    acc = jnp.zeros(x1.shape, F32)
    for c in range(d_ff // fc):
        u = jnp.maximum(jnp.dot(h, w1_ref[:, c * fc:(c + 1) * fc], preferred_element_type=F32), 0.0)
        acc = acc + jnp.dot((u * u).astype(BF16), w2_ref[c * fc:(c + 1) * fc, :],
                            preferred_element_type=F32)
    o_ref[...] = x1 + mod_ref[5:6, :] * acc


def _outproj_mlp(x, o_sb, o_ca, mod, layer, g2, wo_bf, w1_bf, w2_bf):
    b, s, d = x.shape
    d_ff = w1_bf.shape[1]
    tm = min(ROW_TILE, s)
    return pl.pallas_call(
        _mlp_kernel,
        grid=(b, s // tm),
        in_specs=[pl.BlockSpec((None, tm, d), lambda i, j: (i, j, 0)),
                  pl.BlockSpec((None, tm, D_GROUP), lambda i, j: (i, j, 0)),
                  pl.BlockSpec((None, tm, D_GROUP), lambda i, j: (i, j, 0)),
                  pl.BlockSpec((None, None, 6, d), lambda i, j: (layer, i, 0, 0)),
                  _resident((1, d), lambda i, j: (0, 0)),
                  _resident((d, d), lambda i, j: (0, 0)),
                  _resident((d, d_ff), lambda i, j: (0, 0)),
                  _resident((d_ff, d), lambda i, j: (0, 0))],
        out_specs=pl.BlockSpec((None, tm, d), lambda i, j: (i, j, 0)),
        out_shape=jax.ShapeDtypeStruct((b, s, d), F32),
        compiler_params=_cparams(2),
        name="outproj_mlp",
    )(x, o_sb, o_ca, mod, g2, wo_bf, w1_bf, w2_bf)


def kernel(x, c, g_norm1, w_in, g_q, g_k, rel_bias, w_o, g_norm2, w1, w2, w_ada, b_ada):
    depth = w_in.shape[0]
    mod = _ada_modulation(c, w_ada, b_ada)
    bias_tab = _ca_bias_table(rel_bias)
    lane_head = jnp.arange(D_GROUP) // HEAD_DIM
    gmat = ((lane_head[:, None] == lane_head[None, :]).astype(F32) * (1.0 / HEAD_DIM)).astype(BF16)
    for l in range(depth):
        gq_t = (jnp.tile(g_q[l], HEADS_PER_GROUP) * QK_SCALE)[None, :]
        gk_t = jnp.tile(g_k[l], HEADS_PER_GROUP)[None, :]
        proj = _inproj(x, mod, l, g_norm1[l][None, :], w_in[l].astype(BF16), gq_t, gk_t, gmat)
        o_sb = _sb_attention(proj)
        o_ca = _ca_attention(proj, bias_tab, l)
        x = _outproj_mlp(x, o_sb, o_ca, mod, l, g_norm2[l][None, :],
                         w_o[l].astype(BF16), w1[l].astype(BF16), w2[l].astype(BF16))
    return x
```

```python
import jax
import jax.numpy as jnp
from jax import lax
from jax.experimental import pallas as pl
from jax.experimental.pallas import tpu as pltpu

F32 = jnp.float32
BF16 = jnp.bfloat16

HEAD_DIM = 64
LANES = 128
HEADS_PER_GROUP = 8
PAIRS = HEADS_PER_GROUP // 2
D_GROUP = HEADS_PER_GROUP * HEAD_DIM
CHUNK = 64
LEFT_CHUNKS = 8
BAND = (LEFT_CHUNKS + 1) * CHUNK
REL_CLIP = 128
EPS = 1e-6
NEG_INF = -1e30
QK_SCALE = HEAD_DIM ** -0.5
LOG2E = 1.4426950408889634
SOFTPLUS_LINEAR = 64.0

ROW_TILE = 512
INPROJ_ROW_TILE = 1024
SB_TQ = 256
ATTN_TILES = 8
CA_TQ = SB_TQ
CA_WIN = CA_TQ + LEFT_CHUNKS * CHUNK
SB_DEAD = -88.0
VMEM_LIMIT = 56 * 1024 * 1024


def _cparams(n_axes):
    return pltpu.CompilerParams(dimension_semantics=("arbitrary",) * n_axes,
                                vmem_limit_bytes=VMEM_LIMIT)


def _resident(shape, index_map):
    return pl.BlockSpec(shape, index_map, pipeline_mode=pl.Buffered(1))


def _ada_kernel(c_ref, w_ref, b_ref, o_ref):
    ca = jax.nn.silu(c_ref[...]).astype(BF16)
    o_ref[...] = jnp.dot(ca, w_ref[...].astype(BF16), preferred_element_type=F32) + b_ref[...]


def _ada_modulation(c, w_ada, b_ada):
    depth, d, n = w_ada.shape
    b = c.shape[0]
    rows = 8
    c_pad = jnp.pad(c, ((0, rows - b), (0, 0)))
    tn = d
    out = pl.pallas_call(
        _ada_kernel,
        grid=(depth, n // tn),
        in_specs=[pl.BlockSpec((rows, d), lambda l, j: (0, 0)),
                  pl.BlockSpec((None, d, tn), lambda l, j: (l, 0, j)),
                  pl.BlockSpec((None, 1, tn), lambda l, j: (l, 0, j))],
        out_specs=pl.BlockSpec((None, rows, tn), lambda l, j: (l, 0, j)),
        out_shape=jax.ShapeDtypeStruct((depth, rows, n), F32),
        compiler_params=_cparams(2),
        name="ada_modulation",
    )(c_pad, w_ada, b_ada.reshape(depth, 1, n))
    return out[:, :b].reshape(depth, b, 6, d)


def _modulated_norm(x, g, shift, scale):
    ms = jnp.mean(x * x, axis=-1, keepdims=True)
    return (x * lax.rsqrt(ms + EPS) * g) * (1.0 + scale) + shift


def _inproj_kernel(x_ref, mod_ref, g1_ref, w_ref, gq_ref, gk_ref, gmat_ref, o_ref):
    h = _modulated_norm(x_ref[...], g1_ref[...], mod_ref[0:1, :], mod_ref[1:2, :]).astype(BF16)
    for c in range(6):
        cols = slice(c * D_GROUP, (c + 1) * D_GROUP)
        y = jnp.dot(h, w_ref[:, cols], preferred_element_type=F32)
        if c == 0:
            y = y * QK_SCALE
        elif c in (3, 4):
            msq = jnp.dot((y * y).astype(BF16), gmat_ref[...], preferred_element_type=F32)
            y = y * lax.rsqrt(msq + EPS) * (gq_ref[...] if c == 3 else gk_ref[...])
        o_ref[:, cols] = y.astype(BF16)


def _inproj(x, mod, layer, g1, w_in_bf, gq_t, gk_t, gmat):
    b, s, d = x.shape
    n = w_in_bf.shape[1]
    tm = min(INPROJ_ROW_TILE, s)
    return pl.pallas_call(
        _inproj_kernel,
        grid=(b, s // tm),
        in_specs=[pl.BlockSpec((None, tm, d), lambda i, j: (i, j, 0)),
                  pl.BlockSpec((None, None, 6, d), lambda i, j: (layer, i, 0, 0)),
                  _resident((1, d), lambda i, j: (0, 0)),
                  _resident((d, n), lambda i, j: (0, 0)),
                  _resident((1, D_GROUP), lambda i, j: (0, 0)),
                  _resident((1, D_GROUP), lambda i, j: (0, 0)),
                  _resident((D_GROUP, D_GROUP), lambda i, j: (0, 0))],
        out_specs=pl.BlockSpec((None, tm, n), lambda i, j: (i, j, 0)),
        out_shape=jax.ShapeDtypeStruct((b, s, n), BF16),
        compiler_params=_cparams(2),
        name="norm1_inproj",
    )(x, mod, g1, w_in_bf, gq_t, gk_t, gmat)


def _head_masked(q):
    lane = lax.broadcasted_iota(jnp.int32, q.shape, 1)
    zero = jnp.zeros_like(q)
    return jnp.where(lane < HEAD_DIM, q, zero), jnp.where(lane >= HEAD_DIM, q, zero)


def _scores_t(k_blk, q_masked):
    return lax.dot_general(k_blk, q_masked, (((1,), (1,)), ((), ())), preferred_element_type=F32)


def _fill_vt(v_ref, vt_ref, blk):
    for i in range(vt_ref.shape[0]):
        vt_ref[i] = v_ref[i * blk:(i + 1) * blk, :].astype(F32).T.astype(BF16)


def _merge_heads_t(out_a, out_b):
    row = lax.broadcasted_iota(jnp.int32, out_a.shape, 0)
    return jnp.where(row < HEAD_DIM, out_a, out_b).T


def _split_bf16(x):
    hi = lax.bitcast_convert_type(lax.bitcast_convert_type(x, jnp.uint32) & jnp.uint32(0xFFFF0000), F32)
    return hi.astype(BF16), (x - hi).astype(BF16)


def _sb_part(q_ref, k_ref, v_ref, o_ref, vt_ref, acc_ref, with_region):
    tq = vt_ref.shape[2]
    tk = tq
    n_tiles = q_ref.shape[0] // tq
    qi = pl.program_id(2)

    @pl.when(qi == 0)
    def _():
        _fill_vt(v_ref, vt_ref, tk)

    q_heads = [_head_masked(q_ref[t * tq:(t + 1) * tq, :]) for t in range(n_tiles)]
    row = lax.broadcasted_iota(jnp.int32, (tk, tq), 0)
    lane = lax.broadcasted_iota(jnp.int32, (tk, tq), 1)
    tri = (lane >= row).astype(BF16)
    strict = row < lane

    def blocks(work, carries, first):
        depth = max(len(kbs) for kbs in work.values())
        chains = [(t, n, h) for n in range(depth) for t in work if n < len(work[t]) for h in range(2)]
        k_blk = lambda kb: k_ref[pl.ds(pl.multiple_of(kb * tk, tk), tk), :]
        z = {(t, n, h): _scores_t(k_blk(work[t][n]), q_heads[t][h]) for t, n, h in chains}
        incl = {}
        carries = {t: list(carries[t]) for t in work}
        pv = {}

        def suffix_stage(c):
            if first and c[1] == 0:
                z[c] = jnp.where(strict, z[c], NEG_INF)
            sp = jnp.maximum(z[c], jnp.log(1.0 + jnp.exp2(jnp.minimum(z[c], SOFTPLUS_LINEAR) * LOG2E)))
            hi, lo = _split_bf16(sp)
            incl[c] = (jnp.dot(tri, hi, preferred_element_type=F32) +
                       jnp.dot(tri, lo, preferred_element_type=F32))

        def weight_stage(c):
            t, n, h = c
            log_w = z[c] - incl[c]
            if not (first and n == 0):
                log_w = log_w + carries[t][h]
            out = jnp.dot(vt_ref[work[t][n]], jnp.exp(log_w).astype(BF16), preferred_element_type=F32)
            pv[t, h] = out if (t, h) not in pv else pv[t, h] + out
            carries[t][h] = carries[t][h] - incl[c][0:1, :]

        for c in chains:
            suffix_stage(c)
        for c in chains:
            weight_stage(c)
        for t, h in pv:
            if first:
                acc_ref[t, h] = pv[t, h]
            else:
                acc_ref[t, h] += pv[t, h]
        return tuple(tuple(carries[t]) for t in work)

    zero = jnp.zeros((1, tq), F32)
    zeros = {t: (zero, zero) for t in range(n_tiles)}
    base = qi * n_tiles
    full = {t: [base + t, base + t - 1] for t in range(n_tiles)}
    head = dict(full)
    head[0] = [base]

    def region(first_step):
        with_region(first_step)
        return blocks(head if first_step else full, zeros, first=True)

    carries = lax.cond(qi == 0, lambda: region(True), lambda: region(False))

    def cond(state):
        kb, ca, cb = state
        alive = jnp.max(jnp.maximum(ca, cb)) > SB_DEAD
        return jnp.logical_and(kb >= 0, alive)

    least_dead = carries[0][0]
    for t in range(n_tiles):
        for h in range(2):
            least_dead = jnp.maximum(least_dead, carries[t][h])

    @pl.when(jnp.max(least_dead) > SB_DEAD)
    def _():
        for t in range(n_tiles):
            def body(state, t=t):
                kb, ca, cb = state
                ((ca, cb),) = blocks({t: [kb]}, {t: (ca, cb)}, first=False)
                return kb - 1, ca, cb

            lax.while_loop(cond, body, (base + t - 2,) + carries[t])

    for t in range(n_tiles):
        o_ref[t * tq:(t + 1) * tq, :] = _merge_heads_t(acc_ref[t, 0], acc_ref[t, 1]).astype(o_ref.dtype)


def _ca_band_rows(i, half, tq):
    start = half * LANES
    stop = (half + 1) * LANES - CHUNK + BAND
    lo = min(max(start, i * tq), (i + 1) * tq) - i * tq
    hi = max(min(stop, (i + 1) * tq), i * tq) - i * tq
    return lo, max(hi, lo)


def _ca_part(q_ref, k_ref, v_ref, bias_ref, o_ref, vt_ref, s_ref):
    tq = vt_ref.shape[3]
    n_tiles = q_ref.shape[0] // tq
    n_win = CA_WIN // tq
    qi = pl.program_id(2)

    @pl.when(qi == 0)
    def _():
        row = lax.broadcasted_iota(jnp.int32, (LANES, tq), 0)
        for i in range(vt_ref.shape[1]):
            vt = v_ref[i * tq:(i + 1) * tq, :].astype(F32).T
            vt_ref[0, i] = jnp.where(row < HEAD_DIM, vt, 1.0).astype(BF16)
            vt_ref[1, i] = jnp.where(row >= HEAD_DIM, vt, 1.0).astype(BF16)

    units = [(t, h) for t in range(n_tiles) for h in range(2)]
    halves = [slice(f * LANES, (f + 1) * LANES) for f in range(tq // LANES)]
    q_heads = [_head_masked(q_ref[t * tq:(t + 1) * tq, :]) for t in range(n_tiles)]
    first_blk = [qi * n_tiles + t - (n_win - 1) for t in range(n_tiles)]
    blks = [[jnp.maximum(first_blk[t] + i, 0) for i in range(n_win)] for t in range(n_tiles)]
    k_blk = lambda bi: k_ref[pl.ds(pl.multiple_of(bi * tq, tq), tq), :]

    def attend(sequence_start):
        col_max = {}
        for t, h in units:
            for i in range(n_win):
                raw = _scores_t(k_blk(blks[t][i]), q_heads[t][h])
                for f, lanes in enumerate(halves):
                    lo, hi = _ca_band_rows(i, f, tq)
                    if hi == lo:
                        continue
                    rows = slice(i * tq + lo, i * tq + hi)
                    sc = raw[lo:hi, lanes] + bias_ref[h, rows, lanes]
                    if sequence_start and t + i < n_win - 1:
                        sc = jnp.where(first_blk[t] + i >= 0, sc, NEG_INF)
                    s_ref[t, h, rows, lanes] = sc
                    blk_max = sc.max(axis=0, keepdims=True)
                    col_max[t, h, f] = (jnp.maximum(col_max[t, h, f], blk_max)
                                        if (t, h, f) in col_max else blk_max)
        for t in range(n_tiles):
            outs = []
            for h in range(2):
                acc = None
                for i in range(n_win):
                    cols = []
                    for f, lanes in enumerate(halves):
                        lo, hi = _ca_band_rows(i, f, tq)
                        parts = [jnp.zeros((lo, LANES), BF16)] if lo else []
                        if hi > lo:
                            rows = slice(i * tq + lo, i * tq + hi)
                            parts.append(jnp.exp(s_ref[t, h, rows, lanes] - col_max[t, h, f]).astype(BF16))
                        if hi < tq:
                            parts.append(jnp.zeros((tq - hi, LANES), BF16))
                        cols.append(jnp.concatenate(parts, axis=0) if len(parts) > 1 else parts[0])
                    p = jnp.concatenate(cols, axis=1)
                    out = jnp.dot(vt_ref[h, blks[t][i]], p, preferred_element_type=F32)
                    acc = out if acc is None else acc + out
                ones_row = HEAD_DIM if h == 0 else 0
                outs.append(acc * (1.0 / acc[ones_row:ones_row + 1, :]))
            o_ref[t * tq:(t + 1) * tq, :] = _merge_heads_t(outs[0], outs[1]).astype(o_ref.dtype)

    assert n_tiles >= n_win - 1
    return attend


def _attn_kernel(qs_ref, ks_ref, vs_ref, qc_ref, kc_ref, vc_ref, bias_ref, osb_ref, oca_ref,
                 vts_ref, acc_ref, vtc_ref, s_ref):
    ca_attend = _ca_part(qc_ref, kc_ref, vc_ref, bias_ref, oca_ref, vtc_ref, s_ref)
    _sb_part(qs_ref, ks_ref, vs_ref, osb_ref, vts_ref, acc_ref, ca_attend)


def _attention(proj, bias_tab, layer):
    b, s, _ = proj.shape
    tq = SB_TQ
    step = min(ATTN_TILES * tq, s)
    col = lambda group: (lambda i, p, j: (i, 0, group * PAIRS + p))
    tile = lambda group: (lambda i, p, j: (i, j, group * PAIRS + p))
    out_shape = jax.ShapeDtypeStruct((b, s, D_GROUP), BF16)
    return pl.pallas_call(
        _attn_kernel,
        grid=(b, PAIRS, s // step),
        in_specs=[pl.BlockSpec((None, step, LANES), tile(0)),
                  pl.BlockSpec((None, s, LANES), col(1)),
                  pl.BlockSpec((None, s, LANES), col(2)),
                  pl.BlockSpec((None, step, LANES), tile(3)),
                  pl.BlockSpec((None, s, LANES), col(4)),
                  pl.BlockSpec((None, s, LANES), col(5)),
                  pl.BlockSpec((2, CA_WIN, tq), lambda i, p, j: (layer * PAIRS + p, 0, 0))],
        out_specs=[pl.BlockSpec((None, step, LANES), lambda i, p, j: (i, j, p)),
                   pl.BlockSpec((None, step, LANES), lambda i, p, j: (i, j, p))],
        out_shape=[out_shape, out_shape],
        scratch_shapes=[pltpu.VMEM((s // tq, LANES, tq), BF16),
                        pltpu.VMEM((step // tq, 2, LANES, tq), F32),
                        pltpu.VMEM((2, s // tq, LANES, tq), BF16),
                        pltpu.VMEM((step // tq, 2, CA_WIN, tq), F32)],
        compiler_params=_cparams(3),
        name="attention",
    )(proj, proj, proj, proj, proj, proj, bias_tab)


BIAS_ROWS = 128
BIAS_EXT = CA_WIN + CA_TQ


def _ca_bias_kernel(ext_ref, o_ref):
    width = CA_TQ + BIAS_ROWS
    r_loc = lax.broadcasted_iota(jnp.int32, (BIAS_ROWS, CA_TQ), 0)
    col = lax.broadcasted_iota(jnp.int32, (BIAS_ROWS, CA_TQ), 1)
    chunk_start = (col // CHUNK) * CHUNK
    for a in range(CA_WIN // BIAS_ROWS):
        start = CA_WIN - BIAS_ROWS * (a + 1)
        m = jnp.broadcast_to(ext_ref[:, start:start + width], (BIAS_ROWS, width))
        m = pltpu.roll(m, 0, 1, stride=1, stride_axis=0)
        k_in_band = r_loc + (a * BIAS_ROWS) - chunk_start
        in_band = jnp.logical_and(k_in_band >= 0, k_in_band < BAND)
        o_ref[a * BIAS_ROWS:(a + 1) * BIAS_ROWS, :] = jnp.where(
            in_band, m[:, BIAS_ROWS:BIAS_ROWS + CA_TQ], NEG_INF)


def _ca_bias_table(rel_bias):
    depth, n_heads, n_rel = rel_bias.shape
    rb = rel_bias.reshape(depth * n_heads, n_rel).astype(F32)
    n_lo = CA_TQ - REL_CLIP
    n_hi = BIAS_EXT - n_lo - n_rel
    ext = jnp.concatenate([jnp.broadcast_to(rb[:, :1], (rb.shape[0], n_lo)), rb,
                           jnp.broadcast_to(rb[:, -1:], (rb.shape[0], n_hi))], axis=1)
    return pl.pallas_call(
        _ca_bias_kernel,
        grid=(depth * n_heads,),
        in_specs=[pl.BlockSpec((None, 1, BIAS_EXT), lambda i: (i, 0, 0))],
        out_specs=pl.BlockSpec((None, CA_WIN, CA_TQ), lambda i: (i, 0, 0)),
        out_shape=jax.ShapeDtypeStruct((depth * n_heads, CA_WIN, CA_TQ), F32),
        compiler_params=_cparams(1),
        name="ca_bias_table",
    )(ext.reshape(depth * n_heads, 1, BIAS_EXT))


def _mlp_kernel(x_ref, osb_ref, oca_ref, mod_ref, g2_ref, wo_ref, w1_ref, w2_ref, o_ref):
    att = (jnp.dot(osb_ref[...], wo_ref[0:D_GROUP, :], preferred_element_type=F32) +
           jnp.dot(oca_ref[...], wo_ref[D_GROUP:, :], preferred_element_type=F32))
    x1 = x_ref[...] + mod_ref[2:3, :] * att
    h = _modulated_norm(x1, g2_ref[...], mod_ref[3:4, :], mod_ref[4:5, :]).astype(BF16)
    d_ff = w1_ref.shape[1]
    fc = 1024
    acc = jnp.zeros(x1.shape, F32)
    for c in range(d_ff // fc):
        u = jnp.maximum(jnp.dot(h, w1_ref[:, c * fc:(c + 1) * fc], preferred_element_type=F32), 0.0)
        acc = acc + jnp.dot((u * u).astype(BF16), w2_ref[c * fc:(c + 1) * fc, :],
                            preferred_element_type=F32)
    o_ref[...] = x1 + mod_ref[5:6, :] * acc


def _outproj_mlp(x, o_sb, o_ca, mod, layer, g2, wo_bf, w1_bf, w2_bf):
    b, s, d = x.shape
    d_ff = w1_bf.shape[1]
    tm = min(ROW_TILE, s)
    return pl.pallas_call(
        _mlp_kernel,
        grid=(b, s // tm),
        in_specs=[pl.BlockSpec((None, tm, d), lambda i, j: (i, j, 0)),
                  pl.BlockSpec((None, tm, D_GROUP), lambda i, j: (i, j, 0)),
                  pl.BlockSpec((None, tm, D_GROUP), lambda i, j: (i, j, 0)),
                  pl.BlockSpec((None, None, 6, d), lambda i, j: (layer, i, 0, 0)),
                  _resident((1, d), lambda i, j: (0, 0)),
                  _resident((d, d), lambda i, j: (0, 0)),
                  _resident((d, d_ff), lambda i, j: (0, 0)),
                  _resident((d_ff, d), lambda i, j: (0, 0))],
        out_specs=pl.BlockSpec((None, tm, d), lambda i, j: (i, j, 0)),
        out_shape=jax.ShapeDtypeStruct((b, s, d), F32),
        compiler_params=_cparams(2),
        name="outproj_mlp",
    )(x, o_sb, o_ca, mod, g2, wo_bf, w1_bf, w2_bf)


def kernel(x, c, g_norm1, w_in, g_q, g_k, rel_bias, w_o, g_norm2, w1, w2, w_ada, b_ada):
    depth = w_in.shape[0]
    mod = _ada_modulation(c, w_ada, b_ada)
    bias_tab = _ca_bias_table(rel_bias)
    lane_head = jnp.arange(D_GROUP) // HEAD_DIM
    gmat = ((lane_head[:, None] == lane_head[None, :]).astype(F32) * (1.0 / HEAD_DIM)).astype(BF16)
    for l in range(depth):
        gq_t = (jnp.tile(g_q[l], HEADS_PER_GROUP) * QK_SCALE)[None, :]
        gk_t = jnp.tile(g_k[l], HEADS_PER_GROUP)[None, :]
        proj = _inproj(x, mod, l, g_norm1[l][None, :], w_in[l].astype(BF16), gq_t, gk_t, gmat)
        o_sb, o_ca = _attention(proj, bias_tab, l)
        x = _outproj_mlp(x, o_sb, o_ca, mod, l, g_norm2[l][None, :],
                         w_o[l].astype(BF16), w1[l].astype(BF16), w2[l].astype(BF16))
    return x
```

```python
import functools

import jax
import jax.numpy as jnp
from jax import lax
from jax.experimental import pallas as pl
from jax.experimental.pallas import tpu as pltpu

F32 = jnp.float32
BF16 = jnp.bfloat16

HEAD_DIM = 64
LANES = 128
HEADS_PER_GROUP = 8
PAIRS = HEADS_PER_GROUP // 2
D_GROUP = HEADS_PER_GROUP * HEAD_DIM
CHUNK = 64
LEFT_CHUNKS = 8
BAND = (LEFT_CHUNKS + 1) * CHUNK
REL_CLIP = 128
EPS = 1e-6
NEG_INF = -1e30
QK_SCALE = HEAD_DIM ** -0.5
LOG2E = 1.4426950408889634
SOFTPLUS_LINEAR = 64.0

ROW_TILE = 512
INPROJ_ROW_TILE = 1024
SB_TQ = 256
SB_TILES = 4
CA_TQ = 256
CA_TILES = 8
CA_WIN = CA_TQ + LEFT_CHUNKS * CHUNK
SB_DEAD = -88.0
VMEM_LIMIT = 56 * 1024 * 1024
STAGE_BYTES = 2 * 1024 * 1024


def _cparams(n_axes):
    return pltpu.CompilerParams(dimension_semantics=("arbitrary",) * n_axes,
                                vmem_limit_bytes=VMEM_LIMIT)


def _resident(shape, index_map):
    return pl.BlockSpec(shape, index_map, pipeline_mode=pl.Buffered(1))


def _stage_scratch(cols):
    rows = 1 << ((STAGE_BYTES // (4 * cols)).bit_length() - 1)
    return pltpu.VMEM((2, rows, cols), F32), pltpu.SemaphoreType.DMA((2,))


def _stage_weight(src_hbm, dst_ref, stage_ref, sem_ref):
    chunk = stage_ref.shape[1]
    n_chunks = src_hbm.shape[0] // chunk

    def copy(c):
        return pltpu.make_async_copy(src_hbm.at[pl.ds(c * chunk, chunk), :], stage_ref.at[c % 2],
                                     sem_ref.at[c % 2])

    copy(0).start()
    for c in range(n_chunks):
        if c + 1 < n_chunks:
            copy(c + 1).start()
        copy(c).wait()
        dst_ref[c * chunk:(c + 1) * chunk, :] = stage_ref[c % 2].astype(BF16)


def _first_step():
    return jnp.logical_and(pl.program_id(0) == 0, pl.program_id(1) == 0)


def _ada_kernel(c_ref, w_ref, b_ref, o_ref):
    ca = jax.nn.silu(c_ref[...]).astype(BF16)
    o_ref[...] = jnp.dot(ca, w_ref[...].astype(BF16), preferred_element_type=F32) + b_ref[...]


def _ada_modulation(c, w_ada, b_ada):
    depth, d, n = w_ada.shape
    b = c.shape[0]
    rows = 8
    c_pad = jnp.pad(c, ((0, rows - b), (0, 0)))
    tn = d
    out = pl.pallas_call(
        _ada_kernel,
        grid=(depth, n // tn),
        in_specs=[pl.BlockSpec((rows, d), lambda l, j: (0, 0)),
                  pl.BlockSpec((None, d, tn), lambda l, j: (l, 0, j)),
                  pl.BlockSpec((None, 1, tn), lambda l, j: (l, 0, j))],
        out_specs=pl.BlockSpec((None, rows, tn), lambda l, j: (l, 0, j)),
        out_shape=jax.ShapeDtypeStruct((depth, rows, n), F32),
        compiler_params=_cparams(2),
        name="ada_modulation",
    )(c_pad, w_ada, b_ada.reshape(depth, 1, n))
    return out[:, :b].reshape(depth, b, 6, d)


def _modulated_norm(x, g, shift, scale):
    ms = jnp.mean(x * x, axis=-1, keepdims=True)
    return (x * lax.rsqrt(ms + EPS) * g) * (1.0 + scale) + shift


def _inproj_kernel(x_ref, mod_ref, g1_ref, w_hbm, gq_ref, gk_ref, gmat_ref, o_ref,
                   w_ref, stage_ref, sem_ref, *, layer):
    @pl.when(_first_step())
    def _():
        _stage_weight(w_hbm.at[layer], w_ref, stage_ref, sem_ref)

    h = _modulated_norm(x_ref[...], g1_ref[...], mod_ref[0:1, :], mod_ref[1:2, :]).astype(BF16)
    for c in range(6):
        cols = slice(c * D_GROUP, (c + 1) * D_GROUP)
        y = jnp.dot(h, w_ref[:, cols], preferred_element_type=F32)
        if c == 0:
            y = y * QK_SCALE
        elif c in (3, 4):
            msq = jnp.dot((y * y).astype(BF16), gmat_ref[...], preferred_element_type=F32)
            y = y * lax.rsqrt(msq + EPS) * (gq_ref[...] if c == 3 else gk_ref[...])
        o_ref[:, cols] = y.astype(BF16)


def _inproj(x, mod, layer, g1, w_in, gq_t, gk_t, gmat):
    b, s, d = x.shape
    n = w_in.shape[2]
    tm = min(INPROJ_ROW_TILE, s)
    return pl.pallas_call(
        functools.partial(_inproj_kernel, layer=layer),
        grid=(b, s // tm),
        in_specs=[pl.BlockSpec((None, tm, d), lambda i, j: (i, j, 0)),
                  pl.BlockSpec((None, None, 6, d), lambda i, j: (layer, i, 0, 0)),
                  _resident((1, d), lambda i, j: (0, 0)),
                  pl.BlockSpec(memory_space=pl.ANY),
                  _resident((1, D_GROUP), lambda i, j: (0, 0)),
                  _resident((1, D_GROUP), lambda i, j: (0, 0)),
                  _resident((D_GROUP, D_GROUP), lambda i, j: (0, 0))],
        out_specs=pl.BlockSpec((None, tm, n), lambda i, j: (i, j, 0)),
        out_shape=jax.ShapeDtypeStruct((b, s, n), BF16),
        scratch_shapes=[pltpu.VMEM((d, n), BF16), *_stage_scratch(n)],
        compiler_params=_cparams(2),
        name="norm1_inproj",
    )(x, mod, g1, w_in, gq_t, gk_t, gmat)


def _head_masked(q):
    lane = lax.broadcasted_iota(jnp.int32, q.shape, 1)
    zero = jnp.zeros_like(q)
    return jnp.where(lane < HEAD_DIM, q, zero), jnp.where(lane >= HEAD_DIM, q, zero)


def _scores_t(k_blk, q_masked):
    return lax.dot_general(k_blk, q_masked, (((1,), (1,)), ((), ())), preferred_element_type=F32)


def _fill_vt(v_ref, vt_ref, blk):
    for i in range(vt_ref.shape[0]):
        vt_ref[i] = v_ref[i * blk:(i + 1) * blk, :].astype(F32).T.astype(BF16)


def _merge_heads_t(out_a, out_b):
    row = lax.broadcasted_iota(jnp.int32, out_a.shape, 0)
    return jnp.where(row < HEAD_DIM, out_a, out_b).T


def _split_bf16(x):
    hi = lax.bitcast_convert_type(lax.bitcast_convert_type(x, jnp.uint32) & jnp.uint32(0xFFFF0000), F32)
    return hi.astype(BF16), (x - hi).astype(BF16)


def _sb_kernel(q_ref, k_ref, v_ref, o_ref, vt_ref, acc_ref):
    tq = vt_ref.shape[2]
    tk = tq
    n_tiles = q_ref.shape[0] // tq
    qi = pl.program_id(2)

    @pl.when(qi == 0)
    def _():
        _fill_vt(v_ref, vt_ref, tk)

    q_heads = [_head_masked(q_ref[t * tq:(t + 1) * tq, :]) for t in range(n_tiles)]
    row = lax.broadcasted_iota(jnp.int32, (tk, tq), 0)
    lane = lax.broadcasted_iota(jnp.int32, (tk, tq), 1)
    tri = (lane >= row).astype(BF16)
    strict = row < lane

    def blocks(work, carries, first):
        depth = max(len(kbs) for kbs in work.values())
        chains = [(t, n, h) for n in range(depth) for t in work if n < len(work[t]) for h in range(2)]
        k_blk = lambda kb: k_ref[pl.ds(pl.multiple_of(kb * tk, tk), tk), :]
        z = {(t, n, h): _scores_t(k_blk(work[t][n]), q_heads[t][h]) for t, n, h in chains}
        incl = {}
        carries = {t: list(carries[t]) for t in work}
        pv = {}

        def suffix_stage(c):
            if first and c[1] == 0:
                z[c] = jnp.where(strict, z[c], NEG_INF)
            sp = jnp.maximum(z[c], jnp.log(1.0 + jnp.exp2(jnp.minimum(z[c], SOFTPLUS_LINEAR) * LOG2E)))
            hi, lo = _split_bf16(sp)
            incl[c] = (jnp.dot(tri, hi, preferred_element_type=F32) +
                       jnp.dot(tri, lo, preferred_element_type=F32))

        def weight_stage(c):
            t, n, h = c
            log_w = z[c] - incl[c]
            if not (first and n == 0):
                log_w = log_w + carries[t][h]
            out = jnp.dot(vt_ref[work[t][n]], jnp.exp(log_w).astype(BF16), preferred_element_type=F32)
            pv[t, h] = out if (t, h) not in pv else pv[t, h] + out
            carries[t][h] = carries[t][h] - incl[c][0:1, :]

        for c in chains:
            suffix_stage(c)
        for c in chains:
            weight_stage(c)
        for t, h in pv:
            if first:
                acc_ref[t, h] = pv[t, h]
            else:
                acc_ref[t, h] += pv[t, h]
        return tuple(tuple(carries[t]) for t in work)

    zero = jnp.zeros((1, tq), F32)
    zeros = {t: (zero, zero) for t in range(n_tiles)}
    base = qi * n_tiles
    full = {t: [base + t, base + t - 1] for t in range(n_tiles)}
    head = dict(full)
    head[0] = [base]
    carries = lax.cond(qi == 0,
                       lambda: blocks(head, zeros, first=True),
                       lambda: blocks(full, zeros, first=True))

    def cond(state):
        kb, ca, cb = state
        alive = jnp.max(jnp.maximum(ca, cb)) > SB_DEAD
        return jnp.logical_and(kb >= 0, alive)

    least_dead = carries[0][0]
    for t in range(n_tiles):
        for h in range(2):
            least_dead = jnp.maximum(least_dead, carries[t][h])

    @pl.when(jnp.max(least_dead) > SB_DEAD)
    def _():
        for t in range(n_tiles):
            def body(state, t=t):
                kb, ca, cb = state
                ((ca, cb),) = blocks({t: [kb]}, {t: (ca, cb)}, first=False)
                return kb - 1, ca, cb

            lax.while_loop(cond, body, (base + t - 2,) + carries[t])

    for t in range(n_tiles):
        o_ref[t * tq:(t + 1) * tq, :] = _merge_heads_t(acc_ref[t, 0], acc_ref[t, 1]).astype(o_ref.dtype)


def _sb_attention(proj):
    b, s, _ = proj.shape
    tq = min(SB_TQ, s)
    step = min(SB_TILES * tq, s)
    return pl.pallas_call(
        _sb_kernel,
        grid=(b, PAIRS, s // step),
        in_specs=[pl.BlockSpec((None, step, LANES), lambda i, p, j: (i, j, p)),
                  pl.BlockSpec((None, s, LANES), lambda i, p, j: (i, 0, PAIRS + p)),
                  pl.BlockSpec((None, s, LANES), lambda i, p, j: (i, 0, 2 * PAIRS + p))],
        out_specs=pl.BlockSpec((None, step, LANES), lambda i, p, j: (i, j, p)),
        out_shape=jax.ShapeDtypeStruct((b, s, D_GROUP), BF16),
        scratch_shapes=[pltpu.VMEM((s // tq, LANES, tq), BF16),
                        pltpu.VMEM((step // tq, 2, LANES, tq), F32)],
        compiler_params=_cparams(3),
        name="stickbreak_attn",
    )(proj, proj, proj)


def _ca_band_rows(i, half, tq):
    start = half * LANES
    stop = (half + 1) * LANES - CHUNK + BAND
    lo = min(max(start, i * tq), (i + 1) * tq) - i * tq
    hi = max(min(stop, (i + 1) * tq), i * tq) - i * tq
    return lo, max(hi, lo)


def _ca_kernel(q_ref, k_ref, v_ref, bias_ref, o_ref, vt_ref, s_ref):
    tq = vt_ref.shape[3]
    n_tiles = q_ref.shape[0] // tq
    n_win = CA_WIN // tq
    qi = pl.program_id(2)

    @pl.when(qi == 0)
    def _():
        row = lax.broadcasted_iota(jnp.int32, (LANES, tq), 0)
        for i in range(vt_ref.shape[1]):
            vt = v_ref[i * tq:(i + 1) * tq, :].astype(F32).T
            vt_ref[0, i] = jnp.where(row < HEAD_DIM, vt, 1.0).astype(BF16)
            vt_ref[1, i] = jnp.where(row >= HEAD_DIM, vt, 1.0).astype(BF16)

    units = [(t, h) for t in range(n_tiles) for h in range(2)]
    halves = [slice(f * LANES, (f + 1) * LANES) for f in range(tq // LANES)]
    q_heads = [_head_masked(q_ref[t * tq:(t + 1) * tq, :]) for t in range(n_tiles)]
    first_blk = [qi * n_tiles + t - (n_win - 1) for t in range(n_tiles)]
    blks = [[jnp.maximum(first_blk[t] + i, 0) for i in range(n_win)] for t in range(n_tiles)]
    k_blk = lambda bi: k_ref[pl.ds(pl.multiple_of(bi * tq, tq), tq), :]

    def attend(sequence_start):
        col_max = {}
        for t, h in units:
            for i in range(n_win):
                raw = _scores_t(k_blk(blks[t][i]), q_heads[t][h])
                for f, lanes in enumerate(halves):
                    lo, hi = _ca_band_rows(i, f, tq)
                    if hi == lo:
                        continue
                    rows = slice(i * tq + lo, i * tq + hi)
                    sc = raw[lo:hi, lanes] + bias_ref[h, rows, lanes]
                    if sequence_start and t + i < n_win - 1:
                        sc = jnp.where(first_blk[t] + i >= 0, sc, NEG_INF)
                    s_ref[t, h, rows, lanes] = sc
                    blk_max = sc.max(axis=0, keepdims=True)
                    col_max[t, h, f] = (jnp.maximum(col_max[t, h, f], blk_max)
                                        if (t, h, f) in col_max else blk_max)
        for t in range(n_tiles):
            outs = []
            for h in range(2):
                acc = None
                for i in range(n_win):
                    cols = []
                    for f, lanes in enumerate(halves):
                        lo, hi = _ca_band_rows(i, f, tq)
                        parts = [jnp.zeros((lo, LANES), BF16)] if lo else []
                        if hi > lo:
                            rows = slice(i * tq + lo, i * tq + hi)
                            parts.append(jnp.exp(s_ref[t, h, rows, lanes] - col_max[t, h, f]).astype(BF16))
                        if hi < tq:
                            parts.append(jnp.zeros((tq - hi, LANES), BF16))
                        cols.append(jnp.concatenate(parts, axis=0) if len(parts) > 1 else parts[0])
                    p = jnp.concatenate(cols, axis=1)
                    out = jnp.dot(vt_ref[h, blks[t][i]], p, preferred_element_type=F32)
                    acc = out if acc is None else acc + out
                ones_row = HEAD_DIM if h == 0 else 0
                outs.append(acc * (1.0 / acc[ones_row:ones_row + 1, :]))
            o_ref[t * tq:(t + 1) * tq, :] = _merge_heads_t(outs[0], outs[1]).astype(o_ref.dtype)

    n_start_steps = -(-(n_win - 1) // n_tiles)
    pl.when(qi < n_start_steps)(lambda: attend(True))
    pl.when(qi >= n_start_steps)(lambda: attend(False))


def _ca_attention(proj, bias_tab, layer):
    b, s, _ = proj.shape
    tq = CA_TQ
    step = min(CA_TILES * tq, s)
    base = 3 * PAIRS
    return pl.pallas_call(
        _ca_kernel,
        grid=(b, PAIRS, s // step),
        in_specs=[pl.BlockSpec((None, step, LANES), lambda i, p, j: (i, j, base + p)),
                  pl.BlockSpec((None, s, LANES), lambda i, p, j: (i, 0, base + PAIRS + p)),
                  pl.BlockSpec((None, s, LANES), lambda i, p, j: (i, 0, base + 2 * PAIRS + p)),
                  pl.BlockSpec((2, CA_WIN, tq), lambda i, p, j: (layer * PAIRS + p, 0, 0))],
        out_specs=pl.BlockSpec((None, step, LANES), lambda i, p, j: (i, j, p)),
        out_shape=jax.ShapeDtypeStruct((b, s, D_GROUP), BF16),
        scratch_shapes=[pltpu.VMEM((2, s // tq, LANES, tq), BF16),
                        pltpu.VMEM((step // tq, 2, CA_WIN, tq), F32)],
        compiler_params=_cparams(3),
        name="chunkrel_attn",
    )(proj, proj, proj, bias_tab)


BIAS_ROWS = 128
BIAS_EXT = CA_WIN + CA_TQ


def _ca_bias_kernel(ext_ref, o_ref):
    width = CA_TQ + BIAS_ROWS
    r_loc = lax.broadcasted_iota(jnp.int32, (BIAS_ROWS, CA_TQ), 0)
    col = lax.broadcasted_iota(jnp.int32, (BIAS_ROWS, CA_TQ), 1)
    chunk_start = (col // CHUNK) * CHUNK
    for a in range(CA_WIN // BIAS_ROWS):
        start = CA_WIN - BIAS_ROWS * (a + 1)
        m = jnp.broadcast_to(ext_ref[:, start:start + width], (BIAS_ROWS, width))
        m = pltpu.roll(m, 0, 1, stride=1, stride_axis=0)
        k_in_band = r_loc + (a * BIAS_ROWS) - chunk_start
        in_band = jnp.logical_and(k_in_band >= 0, k_in_band < BAND)
        o_ref[a * BIAS_ROWS:(a + 1) * BIAS_ROWS, :] = jnp.where(
            in_band, m[:, BIAS_ROWS:BIAS_ROWS + CA_TQ], NEG_INF)


def _ca_bias_table(rel_bias):
    depth, n_heads, n_rel = rel_bias.shape
    rb = rel_bias.reshape(depth * n_heads, n_rel).astype(F32)
    n_lo = CA_TQ - REL_CLIP
    n_hi = BIAS_EXT - n_lo - n_rel
    ext = jnp.concatenate([jnp.broadcast_to(rb[:, :1], (rb.shape[0], n_lo)), rb,
                           jnp.broadcast_to(rb[:, -1:], (rb.shape[0], n_hi))], axis=1)
    return pl.pallas_call(
        _ca_bias_kernel,
        grid=(depth * n_heads,),
        in_specs=[pl.BlockSpec((None, 1, BIAS_EXT), lambda i: (i, 0, 0))],
        out_specs=pl.BlockSpec((None, CA_WIN, CA_TQ), lambda i: (i, 0, 0)),
        out_shape=jax.ShapeDtypeStruct((depth * n_heads, CA_WIN, CA_TQ), F32),
        compiler_params=_cparams(1),
        name="ca_bias_table",
    )(ext.reshape(depth * n_heads, 1, BIAS_EXT))


def _mlp_kernel(x_ref, osb_ref, oca_ref, mod_ref, g2_ref, wo_hbm, w1_hbm, w2_hbm, o_ref,
                wo_ref, w1_ref, w2_ref, wide_ref, wide_sem, tall_ref, tall_sem, *, layer):
    @pl.when(_first_step())
    def _():
        _stage_weight(w1_hbm.at[layer], w1_ref, wide_ref, wide_sem)
        _stage_weight(w2_hbm.at[layer], w2_ref, tall_ref, tall_sem)
        _stage_weight(wo_hbm.at[layer], wo_ref, tall_ref, tall_sem)

    att = (jnp.dot(osb_ref[...], wo_ref[0:D_GROUP, :], preferred_element_type=F32) +
           jnp.dot(oca_ref[...], wo_ref[D_GROUP:, :], preferred_element_type=F32))
    x1 = x_ref[...] + mod_ref[2:3, :] * att
    h = _modulated_norm(x1, g2_ref[...], mod_ref[3:4, :], mod_ref[4:5, :]).astype(BF16)
    d_ff = w1_ref.shape[1]
    fc = 1024
    acc = jnp.zeros(x1.shape, F32)
    for c in range(d_ff // fc):
        u = jnp.maximum(jnp.dot(h, w1_ref[:, c * fc:(c + 1) * fc], preferred_element_type=F32), 0.0)
        acc = acc + jnp.dot((u * u).astype(BF16), w2_ref[c * fc:(c + 1) * fc, :],
                            preferred_element_type=F32)
    o_ref[...] = x1 + mod_ref[5:6, :] * acc


def _outproj_mlp(x, o_sb, o_ca, mod, layer, g2, w_o, w1, w2):
    b, s, d = x.shape
    d_ff = w1.shape[2]
    tm = min(ROW_TILE, s)
    hbm = pl.BlockSpec(memory_space=pl.ANY)
    return pl.pallas_call(
        functools.partial(_mlp_kernel, layer=layer),
        grid=(b, s // tm),
        in_specs=[pl.BlockSpec((None, tm, d), lambda i, j: (i, j, 0)),
                  pl.BlockSpec((None, tm, D_GROUP), lambda i, j: (i, j, 0)),
                  pl.BlockSpec((None, tm, D_GROUP), lambda i, j: (i, j, 0)),
                  pl.BlockSpec((None, None, 6, d), lambda i, j: (layer, i, 0, 0)),
                  _resident((1, d), lambda i, j: (0, 0)),
                  hbm, hbm, hbm],
        out_specs=pl.BlockSpec((None, tm, d), lambda i, j: (i, j, 0)),
        out_shape=jax.ShapeDtypeStruct((b, s, d), F32),
        scratch_shapes=[pltpu.VMEM((d, d), BF16), pltpu.VMEM((d, d_ff), BF16), pltpu.VMEM((d_ff, d), BF16),
                        *_stage_scratch(d_ff), *_stage_scratch(d)],
        compiler_params=_cparams(2),
        name="outproj_mlp",
    )(x, o_sb, o_ca, mod, g2, w_o, w1, w2)


def kernel(x, c, g_norm1, w_in, g_q, g_k, rel_bias, w_o, g_norm2, w1, w2, w_ada, b_ada):
    depth = w_in.shape[0]
    mod = _ada_modulation(c, w_ada, b_ada)
    bias_tab = _ca_bias_table(rel_bias)
    lane_head = jnp.arange(D_GROUP) // HEAD_DIM
    gmat = ((lane_head[:, None] == lane_head[None, :]).astype(F32) * (1.0 / HEAD_DIM)).astype(BF16)
    for l in range(depth):
        gq_t = (jnp.tile(g_q[l], HEADS_PER_GROUP) * QK_SCALE)[None, :]
        gk_t = jnp.tile(g_k[l], HEADS_PER_GROUP)[None, :]
        proj = _inproj(x, mod, l, g_norm1[l][None, :], w_in, gq_t, gk_t, gmat)
        o_sb = _sb_attention(proj)
        o_ca = _ca_attention(proj, bias_tab, l)
        x = _outproj_mlp(x, o_sb, o_ca, mod, l, g_norm2[l][None, :], w_o, w1, w2)
    return x
```

```python
import functools

import jax
import jax.numpy as jnp
from jax import lax
from jax.experimental import pallas as pl
from jax.experimental.pallas import tpu as pltpu

F32 = jnp.float32
BF16 = jnp.bfloat16

HEAD_DIM = 64
LANES = 128
HEADS_PER_GROUP = 8
PAIRS = HEADS_PER_GROUP // 2
D_GROUP = HEADS_PER_GROUP * HEAD_DIM
CHUNK = 64
LEFT_CHUNKS = 8
BAND = (LEFT_CHUNKS + 1) * CHUNK
REL_CLIP = 128
EPS = 1e-6
NEG_INF = -1e30
QK_SCALE = HEAD_DIM ** -0.5
LOG2E = 1.4426950408889634
SOFTPLUS_LINEAR = 64.0

ROW_TILE = 512
INPROJ_ROW_TILE = 1024
SB_TQ = 256
SB_TILES = 4
CA_TQ = 256
CA_TILES = 8
CA_WIN = CA_TQ + LEFT_CHUNKS * CHUNK
SB_DEAD = -88.0
VMEM_LIMIT = 56 * 1024 * 1024


def _cparams(n_axes):
    return pltpu.CompilerParams(dimension_semantics=("arbitrary",) * n_axes,
                                vmem_limit_bytes=VMEM_LIMIT)


def _resident(shape, index_map):
    return pl.BlockSpec(shape, index_map, pipeline_mode=pl.Buffered(1))


def _stage_scratch(rows, cols):
    return pltpu.VMEM((2, rows, cols), F32), pltpu.SemaphoreType.DMA((2,))


class _WeightStager:
    def __init__(self, blocks, stage_ref, sem_ref):
        self.blocks, self.stage_ref, self.sem_ref = blocks, stage_ref, sem_ref

    def _copy(self, k):
        return pltpu.make_async_copy(self.blocks[k][0], self.stage_ref.at[k % 2], self.sem_ref.at[k % 2])

    def start(self):
        self._copy(0).start()

    def take(self, k):
        if k + 1 < len(self.blocks):
            self._copy(k + 1).start()
        self._copy(k).wait()
        self.blocks[k][1](self.stage_ref[k % 2].astype(BF16))


def _ref_setter(ref, idx):
    def store(value):
        ref[idx] = value
    return store


def _first_step():
    return jnp.logical_and(pl.program_id(0) == 0, pl.program_id(1) == 0)


def _ada_kernel(c_ref, w_ref, b_ref, o_ref):
    ca = jax.nn.silu(c_ref[...]).astype(BF16)
    o_ref[...] = jnp.dot(ca, w_ref[...].astype(BF16), preferred_element_type=F32) + b_ref[...]


def _ada_modulation(c, w_ada, b_ada):
    depth, d, n = w_ada.shape
    b = c.shape[0]
    rows = 8
    c_pad = jnp.pad(c, ((0, rows - b), (0, 0)))
    tn = d
    out = pl.pallas_call(
        _ada_kernel,
        grid=(depth, n // tn),
        in_specs=[pl.BlockSpec((rows, d), lambda l, j: (0, 0)),
                  pl.BlockSpec((None, d, tn), lambda l, j: (l, 0, j)),
                  pl.BlockSpec((None, 1, tn), lambda l, j: (l, 0, j))],
        out_specs=pl.BlockSpec((None, rows, tn), lambda l, j: (l, 0, j)),
        out_shape=jax.ShapeDtypeStruct((depth, rows, n), F32),
        compiler_params=_cparams(2),
        name="ada_modulation",
    )(c_pad, w_ada, b_ada.reshape(depth, 1, n))
    return out[:, :b].reshape(depth, b, 6, d)


def _modulated_norm(x, g, shift, scale):
    ms = jnp.mean(x * x, axis=-1, keepdims=True)
    return (x * lax.rsqrt(ms + EPS) * g) * (1.0 + scale) + shift


def _inproj_kernel(x_ref, mod_ref, g1_ref, w_hbm, gq_ref, gk_ref, gmat_ref, o_ref,
                   w_ref, stage_ref, sem_ref, *, layer):
    groups = [slice(c * D_GROUP, (c + 1) * D_GROUP) for c in range(6)]

    def body(stager):
        if stager:
            stager.start()
        h = _modulated_norm(x_ref[...], g1_ref[...], mod_ref[0:1, :], mod_ref[1:2, :]).astype(BF16)
        for c, cols in enumerate(groups):
            if stager:
                stager.take(c)
            y = jnp.dot(h, w_ref[:, cols], preferred_element_type=F32)
            if c == 0:
                y = y * QK_SCALE
            elif c in (3, 4):
                msq = jnp.dot((y * y).astype(BF16), gmat_ref[...], preferred_element_type=F32)
                y = y * lax.rsqrt(msq + EPS) * (gq_ref[...] if c == 3 else gk_ref[...])
            o_ref[:, cols] = y.astype(BF16)

    w_l = w_hbm.at[layer]
    stager = _WeightStager([(w_l.at[:, cols], _ref_setter(w_ref, (slice(None), cols))) for cols in groups],
                           stage_ref, sem_ref)
    pl.when(_first_step())(lambda: body(stager))
    pl.when(jnp.logical_not(_first_step()))(lambda: body(None))


def _inproj(x, mod, layer, g1, w_in, gq_t, gk_t, gmat):
    b, s, d = x.shape
    n = w_in.shape[2]
    tm = min(INPROJ_ROW_TILE, s)
    return pl.pallas_call(
        functools.partial(_inproj_kernel, layer=layer),
        grid=(b, s // tm),
        in_specs=[pl.BlockSpec((None, tm, d), lambda i, j: (i, j, 0)),
                  pl.BlockSpec((None, None, 6, d), lambda i, j: (layer, i, 0, 0)),
                  _resident((1, d), lambda i, j: (0, 0)),
                  pl.BlockSpec(memory_space=pl.ANY),
                  _resident((1, D_GROUP), lambda i, j: (0, 0)),
                  _resident((1, D_GROUP), lambda i, j: (0, 0)),
                  _resident((D_GROUP, D_GROUP), lambda i, j: (0, 0))],
        out_specs=pl.BlockSpec((None, tm, n), lambda i, j: (i, j, 0)),
        out_shape=jax.ShapeDtypeStruct((b, s, n), BF16),
        scratch_shapes=[pltpu.VMEM((d, n), BF16), *_stage_scratch(d, D_GROUP)],
        compiler_params=_cparams(2),
        name="norm1_inproj",
    )(x, mod, g1, w_in, gq_t, gk_t, gmat)


def _head_masked(q):
    lane = lax.broadcasted_iota(jnp.int32, q.shape, 1)
    zero = jnp.zeros_like(q)
    return jnp.where(lane < HEAD_DIM, q, zero), jnp.where(lane >= HEAD_DIM, q, zero)


def _scores_t(k_blk, q_masked):
    return lax.dot_general(k_blk, q_masked, (((1,), (1,)), ((), ())), preferred_element_type=F32)


def _fill_vt(v_ref, vt_ref, blk):
    for i in range(vt_ref.shape[0]):
        vt_ref[i] = v_ref[i * blk:(i + 1) * blk, :].astype(F32).T.astype(BF16)


def _merge_heads_t(out_a, out_b):
    row = lax.broadcasted_iota(jnp.int32, out_a.shape, 0)
    return jnp.where(row < HEAD_DIM, out_a, out_b).T


def _split_bf16(x):
    hi = lax.bitcast_convert_type(lax.bitcast_convert_type(x, jnp.uint32) & jnp.uint32(0xFFFF0000), F32)
    return hi.astype(BF16), (x - hi).astype(BF16)


def _sb_kernel(q_ref, k_ref, v_ref, o_ref, vt_ref, acc_ref):
    tq = vt_ref.shape[2]
    tk = tq
    n_tiles = q_ref.shape[0] // tq
    qi = pl.program_id(2)

    @pl.when(qi == 0)
    def _():
        _fill_vt(v_ref, vt_ref, tk)

    q_heads = [_head_masked(q_ref[t * tq:(t + 1) * tq, :]) for t in range(n_tiles)]
    row = lax.broadcasted_iota(jnp.int32, (tk, tq), 0)
    lane = lax.broadcasted_iota(jnp.int32, (tk, tq), 1)
    tri = (lane >= row).astype(BF16)
    strict = row < lane

    def blocks(work, carries, first):
        depth = max(len(kbs) for kbs in work.values())
        chains = [(t, n, h) for n in range(depth) for t in work if n < len(work[t]) for h in range(2)]
        k_blk = lambda kb: k_ref[pl.ds(pl.multiple_of(kb * tk, tk), tk), :]
        z = {(t, n, h): _scores_t(k_blk(work[t][n]), q_heads[t][h]) for t, n, h in chains}
        incl = {}
        carries = {t: list(carries[t]) for t in work}
        pv = {}

        def suffix_stage(c):
            if first and c[1] == 0:
                z[c] = jnp.where(strict, z[c], NEG_INF)
            sp = jnp.maximum(z[c], jnp.log(1.0 + jnp.exp2(jnp.minimum(z[c], SOFTPLUS_LINEAR) * LOG2E)))
            hi, lo = _split_bf16(sp)
            incl[c] = (jnp.dot(tri, hi, preferred_element_type=F32) +
                       jnp.dot(tri, lo, preferred_element_type=F32))

        def weight_stage(c):
            t, n, h = c
            log_w = z[c] - incl[c]
            if not (first and n == 0):
                log_w = log_w + carries[t][h]
            out = jnp.dot(vt_ref[work[t][n]], jnp.exp(log_w).astype(BF16), preferred_element_type=F32)
            pv[t, h] = out if (t, h) not in pv else pv[t, h] + out
            carries[t][h] = carries[t][h] - incl[c][0:1, :]

        for c in chains:
            suffix_stage(c)
        for c in chains:
            weight_stage(c)
        for t, h in pv:
            if first:
                acc_ref[t, h] = pv[t, h]
            else:
                acc_ref[t, h] += pv[t, h]
        return tuple(tuple(carries[t]) for t in work)

    zero = jnp.zeros((1, tq), F32)
    zeros = {t: (zero, zero) for t in range(n_tiles)}
    base = qi * n_tiles
    full = {t: [base + t, base + t - 1] for t in range(n_tiles)}
    head = dict(full)
    head[0] = [base]
    carries = lax.cond(qi == 0,
                       lambda: blocks(head, zeros, first=True),
                       lambda: blocks(full, zeros, first=True))

    def cond(state):
        kb, ca, cb = state
        alive = jnp.max(jnp.maximum(ca, cb)) > SB_DEAD
        return jnp.logical_and(kb >= 0, alive)

    least_dead = carries[0][0]
    for t in range(n_tiles):
        for h in range(2):
            least_dead = jnp.maximum(least_dead, carries[t][h])

    @pl.when(jnp.max(least_dead) > SB_DEAD)
    def _():
        for t in range(n_tiles):
            def body(state, t=t):
                kb, ca, cb = state
                ((ca, cb),) = blocks({t: [kb]}, {t: (ca, cb)}, first=False)
                return kb - 1, ca, cb

            lax.while_loop(cond, body, (base + t - 2,) + carries[t])

    for t in range(n_tiles):
        o_ref[t * tq:(t + 1) * tq, :] = _merge_heads_t(acc_ref[t, 0], acc_ref[t, 1]).astype(o_ref.dtype)


def _sb_attention(proj):
    b, s, _ = proj.shape
    tq = min(SB_TQ, s)
    step = min(SB_TILES * tq, s)
    return pl.pallas_call(
        _sb_kernel,
        grid=(b, PAIRS, s // step),
        in_specs=[pl.BlockSpec((None, step, LANES), lambda i, p, j: (i, j, p)),
                  pl.BlockSpec((None, s, LANES), lambda i, p, j: (i, 0, PAIRS + p)),
                  pl.BlockSpec((None, s, LANES), lambda i, p, j: (i, 0, 2 * PAIRS + p))],
        out_specs=pl.BlockSpec((None, step, LANES), lambda i, p, j: (i, j, p)),
        out_shape=jax.ShapeDtypeStruct((b, s, D_GROUP), BF16),
        scratch_shapes=[pltpu.VMEM((s // tq, LANES, tq), BF16),
                        pltpu.VMEM((step // tq, 2, LANES, tq), F32)],
        compiler_params=_cparams(3),
        name="stickbreak_attn",
    )(proj, proj, proj)


def _ca_band_rows(i, half, tq):
    start = half * LANES
    stop = (half + 1) * LANES - CHUNK + BAND
    lo = min(max(start, i * tq), (i + 1) * tq) - i * tq
    hi = max(min(stop, (i + 1) * tq), i * tq) - i * tq
    return lo, max(hi, lo)


def _ca_kernel(q_ref, k_ref, v_ref, bias_ref, o_ref, vt_ref, s_ref):
    tq = vt_ref.shape[3]
    n_tiles = q_ref.shape[0] // tq
    n_win = CA_WIN // tq
    qi = pl.program_id(2)

    @pl.when(qi == 0)
    def _():
        row = lax.broadcasted_iota(jnp.int32, (LANES, tq), 0)
        for i in range(vt_ref.shape[1]):
            vt = v_ref[i * tq:(i + 1) * tq, :].astype(F32).T
            vt_ref[0, i] = jnp.where(row < HEAD_DIM, vt, 1.0).astype(BF16)
            vt_ref[1, i] = jnp.where(row >= HEAD_DIM, vt, 1.0).astype(BF16)

    units = [(t, h) for t in range(n_tiles) for h in range(2)]
    halves = [slice(f * LANES, (f + 1) * LANES) for f in range(tq // LANES)]
    q_heads = [_head_masked(q_ref[t * tq:(t + 1) * tq, :]) for t in range(n_tiles)]
    first_blk = [qi * n_tiles + t - (n_win - 1) for t in range(n_tiles)]
    blks = [[jnp.maximum(first_blk[t] + i, 0) for i in range(n_win)] for t in range(n_tiles)]
    k_blk = lambda bi: k_ref[pl.ds(pl.multiple_of(bi * tq, tq), tq), :]

    def attend(sequence_start):
        col_max = {}
        for t, h in units:
            for i in range(n_win):
                raw = _scores_t(k_blk(blks[t][i]), q_heads[t][h])
                for f, lanes in enumerate(halves):
                    lo, hi = _ca_band_rows(i, f, tq)
                    if hi == lo:
                        continue
                    rows = slice(i * tq + lo, i * tq + hi)
                    sc = raw[lo:hi, lanes] + bias_ref[h, rows, lanes]
                    if sequence_start and t + i < n_win - 1:
                        sc = jnp.where(first_blk[t] + i >= 0, sc, NEG_INF)
                    s_ref[t, h, rows, lanes] = sc
                    blk_max = sc.max(axis=0, keepdims=True)
                    col_max[t, h, f] = (jnp.maximum(col_max[t, h, f], blk_max)
                                        if (t, h, f) in col_max else blk_max)
        for t in range(n_tiles):
            outs = []
            for h in range(2):
                acc = None
                for i in range(n_win):
                    cols = []
                    for f, lanes in enumerate(halves):
                        lo, hi = _ca_band_rows(i, f, tq)
                        parts = [jnp.zeros((lo, LANES), BF16)] if lo else []
                        if hi > lo:
                            rows = slice(i * tq + lo, i * tq + hi)
                            parts.append(jnp.exp(s_ref[t, h, rows, lanes] - col_max[t, h, f]).astype(BF16))
                        if hi < tq:
                            parts.append(jnp.zeros((tq - hi, LANES), BF16))
                        cols.append(jnp.concatenate(parts, axis=0) if len(parts) > 1 else parts[0])
                    p = jnp.concatenate(cols, axis=1)
                    out = jnp.dot(vt_ref[h, blks[t][i]], p, preferred_element_type=F32)
                    acc = out if acc is None else acc + out
                ones_row = HEAD_DIM if h == 0 else 0
                outs.append(acc * (1.0 / acc[ones_row:ones_row + 1, :]))
            o_ref[t * tq:(t + 1) * tq, :] = _merge_heads_t(outs[0], outs[1]).astype(o_ref.dtype)

    n_start_steps = -(-(n_win - 1) // n_tiles)
    pl.when(qi < n_start_steps)(lambda: attend(True))
    pl.when(qi >= n_start_steps)(lambda: attend(False))


def _ca_attention(proj, bias_tab, layer):
    b, s, _ = proj.shape
    tq = CA_TQ
    step = min(CA_TILES * tq, s)
    base = 3 * PAIRS
    return pl.pallas_call(
        _ca_kernel,
        grid=(b, PAIRS, s // step),
        in_specs=[pl.BlockSpec((None, step, LANES), lambda i, p, j: (i, j, base + p)),
                  pl.BlockSpec((None, s, LANES), lambda i, p, j: (i, 0, base + PAIRS + p)),
                  pl.BlockSpec((None, s, LANES), lambda i, p, j: (i, 0, base + 2 * PAIRS + p)),
                  pl.BlockSpec((2, CA_WIN, tq), lambda i, p, j: (layer * PAIRS + p, 0, 0))],
        out_specs=pl.BlockSpec((None, step, LANES), lambda i, p, j: (i, j, p)),
        out_shape=jax.ShapeDtypeStruct((b, s, D_GROUP), BF16),
        scratch_shapes=[pltpu.VMEM((2, s // tq, LANES, tq), BF16),
                        pltpu.VMEM((step // tq, 2, CA_WIN, tq), F32)],
        compiler_params=_cparams(3),
        name="chunkrel_attn",
    )(proj, proj, proj, bias_tab)


BIAS_ROWS = 128
BIAS_EXT = CA_WIN + CA_TQ


def _ca_bias_kernel(ext_ref, o_ref):
    width = CA_TQ + BIAS_ROWS
    r_loc = lax.broadcasted_iota(jnp.int32, (BIAS_ROWS, CA_TQ), 0)
    col = lax.broadcasted_iota(jnp.int32, (BIAS_ROWS, CA_TQ), 1)
    chunk_start = (col // CHUNK) * CHUNK
    for a in range(CA_WIN // BIAS_ROWS):
        start = CA_WIN - BIAS_ROWS * (a + 1)
        m = jnp.broadcast_to(ext_ref[:, start:start + width], (BIAS_ROWS, width))
        m = pltpu.roll(m, 0, 1, stride=1, stride_axis=0)
        k_in_band = r_loc + (a * BIAS_ROWS) - chunk_start
        in_band = jnp.logical_and(k_in_band >= 0, k_in_band < BAND)
        o_ref[a * BIAS_ROWS:(a + 1) * BIAS_ROWS, :] = jnp.where(
            in_band, m[:, BIAS_ROWS:BIAS_ROWS + CA_TQ], NEG_INF)


def _ca_bias_table(rel_bias):
    depth, n_heads, n_rel = rel_bias.shape
    rb = rel_bias.reshape(depth * n_heads, n_rel).astype(F32)
    n_lo = CA_TQ - REL_CLIP
    n_hi = BIAS_EXT - n_lo - n_rel
    ext = jnp.concatenate([jnp.broadcast_to(rb[:, :1], (rb.shape[0], n_lo)), rb,
                           jnp.broadcast_to(rb[:, -1:], (rb.shape[0], n_hi))], axis=1)
    return pl.pallas_call(
        _ca_bias_kernel,
        grid=(depth * n_heads,),
        in_specs=[pl.BlockSpec((None, 1, BIAS_EXT), lambda i: (i, 0, 0))],
        out_specs=pl.BlockSpec((None, CA_WIN, CA_TQ), lambda i: (i, 0, 0)),
        out_shape=jax.ShapeDtypeStruct((depth * n_heads, CA_WIN, CA_TQ), F32),
        compiler_params=_cparams(1),
        name="ca_bias_table",
    )(ext.reshape(depth * n_heads, 1, BIAS_EXT))


def _mlp_kernel(x_ref, osb_ref, oca_ref, mod_ref, g2_ref, wo_hbm, w1_hbm, w2_hbm, o_ref,
                wo_ref, w1_ref, w2_ref, stage_ref, sem_ref, *, layer):
    d, d_ff = w1_ref.shape
    fc = stage_ref.shape[2]
    chunks = [slice(c * fc, (c + 1) * fc) for c in range(d_ff // fc)]

    def body(stager):
        if stager:
            stager.start()
            stager.take(0)
        att = (jnp.dot(osb_ref[...], wo_ref[0:D_GROUP, :], preferred_element_type=F32) +
               jnp.dot(oca_ref[...], wo_ref[D_GROUP:, :], preferred_element_type=F32))
        x1 = x_ref[...] + mod_ref[2:3, :] * att
        h = _modulated_norm(x1, g2_ref[...], mod_ref[3:4, :], mod_ref[4:5, :]).astype(BF16)
        acc = jnp.zeros(x1.shape, F32)
        for c, cols in enumerate(chunks):
            if stager:
                stager.take(1 + 2 * c)
            u = jnp.maximum(jnp.dot(h, w1_ref[:, cols], preferred_element_type=F32), 0.0)
            if stager:
                stager.take(2 + 2 * c)
            acc = acc + jnp.dot((u * u).astype(BF16), w2_ref[cols, :], preferred_element_type=F32)
        o_ref[...] = x1 + mod_ref[5:6, :] * acc

    wo_l, w1_l, w2_l = wo_hbm.at[layer], w1_hbm.at[layer], w2_hbm.at[layer]
    blocks = [(wo_l, _ref_setter(wo_ref, (slice(None), slice(None))))]
    for cols in chunks:
        blocks.append((w1_l.at[:, cols], _ref_setter(w1_ref, (slice(None), cols))))
        blocks.append((w2_l.at[cols, :], _ref_setter(w2_ref, (cols, slice(None)))))
    stager = _WeightStager(blocks, stage_ref, sem_ref)
    pl.when(_first_step())(lambda: body(stager))
    pl.when(jnp.logical_not(_first_step()))(lambda: body(None))


def _outproj_mlp(x, o_sb, o_ca, mod, layer, g2, w_o, w1, w2):
    b, s, d = x.shape
    d_ff = w1.shape[2]
    tm = min(ROW_TILE, s)
    hbm = pl.BlockSpec(memory_space=pl.ANY)
    return pl.pallas_call(
        functools.partial(_mlp_kernel, layer=layer),
        grid=(b, s // tm),
        in_specs=[pl.BlockSpec((None, tm, d), lambda i, j: (i, j, 0)),
                  pl.BlockSpec((None, tm, D_GROUP), lambda i, j: (i, j, 0)),
                  pl.BlockSpec((None, tm, D_GROUP), lambda i, j: (i, j, 0)),
                  pl.BlockSpec((None, None, 6, d), lambda i, j: (layer, i, 0, 0)),
                  _resident((1, d), lambda i, j: (0, 0)),
                  hbm, hbm, hbm],
        out_specs=pl.BlockSpec((None, tm, d), lambda i, j: (i, j, 0)),
        out_shape=jax.ShapeDtypeStruct((b, s, d), F32),
        scratch_shapes=[pltpu.VMEM((d, d), BF16), pltpu.VMEM((d, d_ff), BF16), pltpu.VMEM((d_ff, d), BF16),
                        *_stage_scratch(d, d)],
        compiler_params=_cparams(2),
        name="outproj_mlp",
    )(x, o_sb, o_ca, mod, g2, w_o, w1, w2)


def kernel(x, c, g_norm1, w_in, g_q, g_k, rel_bias, w_o, g_norm2, w1, w2, w_ada, b_ada):
    depth = w_in.shape[0]
    mod = _ada_modulation(c, w_ada, b_ada)
    bias_tab = _ca_bias_table(rel_bias)
    lane_head = jnp.arange(D_GROUP) // HEAD_DIM
    gmat = ((lane_head[:, None] == lane_head[None, :]).astype(F32) * (1.0 / HEAD_DIM)).astype(BF16)
    for l in range(depth):
        gq_t = (jnp.tile(g_q[l], HEADS_PER_GROUP) * QK_SCALE)[None, :]
        gk_t = jnp.tile(g_k[l], HEADS_PER_GROUP)[None, :]
        proj = _inproj(x, mod, l, g_norm1[l][None, :], w_in, gq_t, gk_t, gmat)
        o_sb = _sb_attention(proj)
        o_ca = _ca_attention(proj, bias_tab, l)
        x = _outproj_mlp(x, o_sb, o_ca, mod, l, g_norm2[l][None, :], w_o, w1, w2)
    return x
```

```python
import functools

import jax
import jax.numpy as jnp
from jax import lax
from jax.experimental import pallas as pl
from jax.experimental.pallas import tpu as pltpu

F32 = jnp.float32
BF16 = jnp.bfloat16

HEAD_DIM = 64
LANES = 128
HEADS_PER_GROUP = 8
PAIRS = HEADS_PER_GROUP // 2
D_GROUP = HEADS_PER_GROUP * HEAD_DIM
CHUNK = 64
LEFT_CHUNKS = 8
BAND = (LEFT_CHUNKS + 1) * CHUNK
REL_CLIP = 128
EPS = 1e-6
NEG_INF = -1e30
QK_SCALE = HEAD_DIM ** -0.5
LOG2E = 1.4426950408889634
SOFTPLUS_LINEAR = 64.0

ROW_TILE = 512
INPROJ_ROW_TILE = 1024
SB_TQ = 256
SB_TILES = 4
CA_TQ = 256
CA_TILES = 8
CA_WIN = CA_TQ + LEFT_CHUNKS * CHUNK
SB_DEAD = -88.0
VMEM_LIMIT = 56 * 1024 * 1024


def _cparams(n_axes):
    return pltpu.CompilerParams(dimension_semantics=("arbitrary",) * n_axes,
                                vmem_limit_bytes=VMEM_LIMIT)


def _resident(shape, index_map):
    return pl.BlockSpec(shape, index_map, pipeline_mode=pl.Buffered(1))


def _stage_scratch(rows, cols):
    return pltpu.VMEM((2, rows, cols), F32), pltpu.SemaphoreType.DMA((2,))


class _WeightStager:
    def __init__(self, blocks, stage_ref, sem_ref):
        self.blocks, self.stage_ref, self.sem_ref = blocks, stage_ref, sem_ref

    def _copy(self, k):
        return pltpu.make_async_copy(self.blocks[k][0], self.stage_ref.at[k % 2], self.sem_ref.at[k % 2])

    def start(self):
        self._copy(0).start()

    def take(self, k):
        if k + 1 < len(self.blocks):
            self._copy(k + 1).start()
        self._copy(k).wait()
        self.blocks[k][1](self.stage_ref[k % 2].astype(BF16))


def _ref_setter(ref, idx):
    def store(value):
        ref[idx] = value
    return store


def _first_step():
    return jnp.logical_and(pl.program_id(0) == 0, pl.program_id(1) == 0)


def _ada_kernel(c_ref, w_ref, b_ref, o_ref):
    ca = jax.nn.silu(c_ref[...]).astype(BF16)
    o_ref[...] = jnp.dot(ca, w_ref[...].astype(BF16), preferred_element_type=F32) + b_ref[...]


def _ada_modulation(c, w_ada, b_ada):
    depth, d, n = w_ada.shape
    b = c.shape[0]
    rows = 8
    c_pad = jnp.pad(c, ((0, rows - b), (0, 0)))
    tn = n // 2
    out = pl.pallas_call(
        _ada_kernel,
        grid=(depth, n // tn),
        in_specs=[pl.BlockSpec((rows, d), lambda l, j: (0, 0)),
                  pl.BlockSpec((None, d, tn), lambda l, j: (l, 0, j)),
                  pl.BlockSpec((None, 1, tn), lambda l, j: (l, 0, j))],
        out_specs=pl.BlockSpec((None, rows, tn), lambda l, j: (l, 0, j)),
        out_shape=jax.ShapeDtypeStruct((depth, rows, n), F32),
        compiler_params=_cparams(2),
        name="ada_modulation",
    )(c_pad, w_ada, b_ada.reshape(depth, 1, n))
    return out[:, :b].reshape(depth, b, 6, d)


def _modulated_norm(x, g, shift, scale):
    ms = jnp.mean(x * x, axis=-1, keepdims=True)
    return (x * lax.rsqrt(ms + EPS) * g) * (1.0 + scale) + shift


def _inproj_kernel(x_ref, mod_ref, g1_ref, w_hbm, gq_ref, gk_ref, gmat_ref, o_ref,
                   w_ref, stage_ref, sem_ref, *, layer):
    groups = [slice(c * D_GROUP, (c + 1) * D_GROUP) for c in range(6)]

    def body(stager):
        if stager:
            stager.start()
        h = _modulated_norm(x_ref[...], g1_ref[...], mod_ref[0:1, :], mod_ref[1:2, :]).astype(BF16)
        for c, cols in enumerate(groups):
            if stager:
                stager.take(c)
            y = jnp.dot(h, w_ref[:, cols], preferred_element_type=F32)
            if c == 0:
                y = y * QK_SCALE
            elif c in (3, 4):
                msq = jnp.dot((y * y).astype(BF16), gmat_ref[...], preferred_element_type=F32)
                y = y * lax.rsqrt(msq + EPS) * (gq_ref[...] if c == 3 else gk_ref[...])
            o_ref[:, cols] = y.astype(BF16)

    w_l = w_hbm.at[layer]
    stager = _WeightStager([(w_l.at[:, cols], _ref_setter(w_ref, (slice(None), cols))) for cols in groups],
                           stage_ref, sem_ref)
    pl.when(_first_step())(lambda: body(stager))
    pl.when(jnp.logical_not(_first_step()))(lambda: body(None))


def _inproj(x, mod, layer, g1, w_in, gq_t, gk_t, gmat):
    b, s, d = x.shape
    n = w_in.shape[2]
    tm = min(INPROJ_ROW_TILE, s)
    return pl.pallas_call(
        functools.partial(_inproj_kernel, layer=layer),
        grid=(b, s // tm),
        in_specs=[pl.BlockSpec((None, tm, d), lambda i, j: (i, j, 0)),
                  pl.BlockSpec((None, None, 6, d), lambda i, j: (layer, i, 0, 0)),
                  _resident((1, d), lambda i, j: (0, 0)),
                  pl.BlockSpec(memory_space=pl.ANY),
                  _resident((1, D_GROUP), lambda i, j: (0, 0)),
                  _resident((1, D_GROUP), lambda i, j: (0, 0)),
                  _resident((D_GROUP, D_GROUP), lambda i, j: (0, 0))],
        out_specs=pl.BlockSpec((None, tm, n), lambda i, j: (i, j, 0)),
        out_shape=jax.ShapeDtypeStruct((b, s, n), BF16),
        scratch_shapes=[pltpu.VMEM((d, n), BF16), *_stage_scratch(d, D_GROUP)],
        compiler_params=_cparams(2),
        name="norm1_inproj",
    )(x, mod, g1, w_in, gq_t, gk_t, gmat)


def _head_masked(q):
    lane = lax.broadcasted_iota(jnp.int32, q.shape, 1)
    zero = jnp.zeros_like(q)
    return jnp.where(lane < HEAD_DIM, q, zero), jnp.where(lane >= HEAD_DIM, q, zero)


def _scores_t(k_blk, q_masked):
    return lax.dot_general(k_blk, q_masked, (((1,), (1,)), ((), ())), preferred_element_type=F32)


def _fill_vt(v_ref, vt_ref, blk):
    for i in range(vt_ref.shape[0]):
        vt_ref[i] = v_ref[i * blk:(i + 1) * blk, :].astype(F32).T.astype(BF16)


def _merge_heads_t(out_a, out_b):
    row = lax.broadcasted_iota(jnp.int32, out_a.shape, 0)
    return jnp.where(row < HEAD_DIM, out_a, out_b).T


def _split_bf16(x):
    hi = lax.bitcast_convert_type(lax.bitcast_convert_type(x, jnp.uint32) & jnp.uint32(0xFFFF0000), F32)
    return hi.astype(BF16), (x - hi).astype(BF16)


def _sb_kernel(q_ref, k_ref, v_ref, o_ref, vt_ref, acc_ref):
    tq = vt_ref.shape[2]
    tk = tq
    n_tiles = q_ref.shape[0] // tq
    qi = pl.program_id(2)

    @pl.when(qi == 0)
    def _():
        _fill_vt(v_ref, vt_ref, tk)

    q_heads = [_head_masked(q_ref[t * tq:(t + 1) * tq, :]) for t in range(n_tiles)]
    row = lax.broadcasted_iota(jnp.int32, (tk, tq), 0)
    lane = lax.broadcasted_iota(jnp.int32, (tk, tq), 1)
    tri = (lane >= row).astype(BF16)
    strict = row < lane

    def blocks(work, carries, first):
        depth = max(len(kbs) for kbs in work.values())
        chains = [(t, n, h) for n in range(depth) for t in work if n < len(work[t]) for h in range(2)]
        k_blk = lambda kb: k_ref[pl.ds(pl.multiple_of(kb * tk, tk), tk), :]
        z = {(t, n, h): _scores_t(k_blk(work[t][n]), q_heads[t][h]) for t, n, h in chains}
        incl = {}
        carries = {t: list(carries[t]) for t in work}
        pv = {}

        def suffix_stage(c):
            if first and c[1] == 0:
                z[c] = jnp.where(strict, z[c], NEG_INF)
            sp = jnp.maximum(z[c], jnp.log(1.0 + jnp.exp2(jnp.minimum(z[c], SOFTPLUS_LINEAR) * LOG2E)))
            hi, lo = _split_bf16(sp)
            incl[c] = (jnp.dot(tri, hi, preferred_element_type=F32) +
                       jnp.dot(tri, lo, preferred_element_type=F32))

        def weight_stage(c):
            t, n, h = c
            log_w = z[c] - incl[c]
            if not (first and n == 0):
                log_w = log_w + carries[t][h]
            out = jnp.dot(vt_ref[work[t][n]], jnp.exp(log_w).astype(BF16), preferred_element_type=F32)
            pv[t, h] = out if (t, h) not in pv else pv[t, h] + out
            carries[t][h] = carries[t][h] - incl[c][0:1, :]

        for c in chains:
            suffix_stage(c)
        for c in chains:
            weight_stage(c)
        for t, h in pv:
            if first:
                acc_ref[t, h] = pv[t, h]
            else:
                acc_ref[t, h] += pv[t, h]
        return tuple(tuple(carries[t]) for t in work)

    zero = jnp.zeros((1, tq), F32)
    zeros = {t: (zero, zero) for t in range(n_tiles)}
    base = qi * n_tiles
    full = {t: [base + t, base + t - 1] for t in range(n_tiles)}
    head = dict(full)
    head[0] = [base]
    carries = lax.cond(qi == 0,
                       lambda: blocks(head, zeros, first=True),
                       lambda: blocks(full, zeros, first=True))

    def cond(state):
        kb, ca, cb = state
        alive = jnp.max(jnp.maximum(ca, cb)) > SB_DEAD
        return jnp.logical_and(kb >= 0, alive)

    least_dead = carries[0][0]
    for t in range(n_tiles):
        for h in range(2):
            least_dead = jnp.maximum(least_dead, carries[t][h])

    @pl.when(jnp.max(least_dead) > SB_DEAD)
    def _():
        for t in range(n_tiles):
            def body(state, t=t):
                kb, ca, cb = state
                ((ca, cb),) = blocks({t: [kb]}, {t: (ca, cb)}, first=False)
                return kb - 1, ca, cb

            lax.while_loop(cond, body, (base + t - 2,) + carries[t])

    for t in range(n_tiles):
        o_ref[t * tq:(t + 1) * tq, :] = _merge_heads_t(acc_ref[t, 0], acc_ref[t, 1]).astype(o_ref.dtype)


def _sb_attention(proj):
    b, s, _ = proj.shape
    tq = min(SB_TQ, s)
    step = min(SB_TILES * tq, s)
    return pl.pallas_call(
        _sb_kernel,
        grid=(b, PAIRS, s // step),
        in_specs=[pl.BlockSpec((None, step, LANES), lambda i, p, j: (i, j, p)),
                  pl.BlockSpec((None, s, LANES), lambda i, p, j: (i, 0, PAIRS + p)),
                  pl.BlockSpec((None, s, LANES), lambda i, p, j: (i, 0, 2 * PAIRS + p))],
        out_specs=pl.BlockSpec((None, step, LANES), lambda i, p, j: (i, j, p)),
        out_shape=jax.ShapeDtypeStruct((b, s, D_GROUP), BF16),
        scratch_shapes=[pltpu.VMEM((s // tq, LANES, tq), BF16),
                        pltpu.VMEM((step // tq, 2, LANES, tq), F32)],
        compiler_params=_cparams(3),
        name="stickbreak_attn",
    )(proj, proj, proj)


def _ca_band_rows(i, half, tq):
    start = half * LANES
    stop = (half + 1) * LANES - CHUNK + BAND
    lo = min(max(start, i * tq), (i + 1) * tq) - i * tq
    hi = max(min(stop, (i + 1) * tq), i * tq) - i * tq
    return lo, max(hi, lo)


def _ca_kernel(q_ref, k_ref, v_ref, bias_ref, o_ref, vt_ref, s_ref):
    tq = vt_ref.shape[3]
    n_tiles = q_ref.shape[0] // tq
    n_win = CA_WIN // tq
    qi = pl.program_id(2)

    @pl.when(qi == 0)
    def _():
        row = lax.broadcasted_iota(jnp.int32, (LANES, tq), 0)
        for i in range(vt_ref.shape[1]):
            vt = v_ref[i * tq:(i + 1) * tq, :].astype(F32).T
            vt_ref[0, i] = jnp.where(row < HEAD_DIM, vt, 1.0).astype(BF16)
            vt_ref[1, i] = jnp.where(row >= HEAD_DIM, vt, 1.0).astype(BF16)

    units = [(t, h) for t in range(n_tiles) for h in range(2)]
    halves = [slice(f * LANES, (f + 1) * LANES) for f in range(tq // LANES)]
    q_heads = [_head_masked(q_ref[t * tq:(t + 1) * tq, :]) for t in range(n_tiles)]
    first_blk = [qi * n_tiles + t - (n_win - 1) for t in range(n_tiles)]
    blks = [[jnp.maximum(first_blk[t] + i, 0) for i in range(n_win)] for t in range(n_tiles)]
    k_blk = lambda bi: k_ref[pl.ds(pl.multiple_of(bi * tq, tq), tq), :]

    def attend(sequence_start):
        col_max = {}
        for t, h in units:
            for i in range(n_win):
                raw = _scores_t(k_blk(blks[t][i]), q_heads[t][h])
                for f, lanes in enumerate(halves):
                    lo, hi = _ca_band_rows(i, f, tq)
                    if hi == lo:
                        continue
                    rows = slice(i * tq + lo, i * tq + hi)
                    sc = raw[lo:hi, lanes] + bias_ref[h, rows, lanes]
                    if sequence_start and t + i < n_win - 1:
                        sc = jnp.where(first_blk[t] + i >= 0, sc, NEG_INF)
                    s_ref[t, h, rows, lanes] = sc
                    blk_max = sc.max(axis=0, keepdims=True)
                    col_max[t, h, f] = (jnp.maximum(col_max[t, h, f], blk_max)
                                        if (t, h, f) in col_max else blk_max)
        for t in range(n_tiles):
            outs = []
            for h in range(2):
                acc = None
                for i in range(n_win):
                    cols = []
                    for f, lanes in enumerate(halves):
                        lo, hi = _ca_band_rows(i, f, tq)
                        parts = [jnp.zeros((lo, LANES), BF16)] if lo else []
                        if hi > lo:
                            rows = slice(i * tq + lo, i * tq + hi)
                            parts.append(jnp.exp(s_ref[t, h, rows, lanes] - col_max[t, h, f]).astype(BF16))
                        if hi < tq:
                            parts.append(jnp.zeros((tq - hi, LANES), BF16))
                        cols.append(jnp.concatenate(parts, axis=0) if len(parts) > 1 else parts[0])
                    p = jnp.concatenate(cols, axis=1)
                    out = jnp.dot(vt_ref[h, blks[t][i]], p, preferred_element_type=F32)
                    acc = out if acc is None else acc + out
                ones_row = HEAD_DIM if h == 0 else 0
                outs.append(acc * (1.0 / acc[ones_row:ones_row + 1, :]))
            o_ref[t * tq:(t + 1) * tq, :] = _merge_heads_t(outs[0], outs[1]).astype(o_ref.dtype)

    n_start_steps = -(-(n_win - 1) // n_tiles)
    pl.when(qi < n_start_steps)(lambda: attend(True))
    pl.when(qi >= n_start_steps)(lambda: attend(False))


def _ca_attention(proj, bias_tab, layer):
    b, s, _ = proj.shape
    tq = CA_TQ
    step = min(CA_TILES * tq, s)
    base = 3 * PAIRS
    return pl.pallas_call(
        _ca_kernel,
        grid=(b, PAIRS, s // step),
        in_specs=[pl.BlockSpec((None, step, LANES), lambda i, p, j: (i, j, base + p)),
                  pl.BlockSpec((None, s, LANES), lambda i, p, j: (i, 0, base + PAIRS + p)),
                  pl.BlockSpec((None, s, LANES), lambda i, p, j: (i, 0, base + 2 * PAIRS + p)),
                  pl.BlockSpec((2, CA_WIN, tq), lambda i, p, j: (layer * PAIRS + p, 0, 0))],
        out_specs=pl.BlockSpec((None, step, LANES), lambda i, p, j: (i, j, p)),
        out_shape=jax.ShapeDtypeStruct((b, s, D_GROUP), BF16),
        scratch_shapes=[pltpu.VMEM((2, s // tq, LANES, tq), BF16),
                        pltpu.VMEM((step // tq, 2, CA_WIN, tq), F32)],
        compiler_params=_cparams(3),
        name="chunkrel_attn",
    )(proj, proj, proj, bias_tab)


BIAS_ROWS = 128
BIAS_EXT = CA_WIN + CA_TQ


def _ca_bias_kernel(ext_ref, o_ref):
    width = CA_TQ + BIAS_ROWS
    r_loc = lax.broadcasted_iota(jnp.int32, (BIAS_ROWS, CA_TQ), 0)
    col = lax.broadcasted_iota(jnp.int32, (BIAS_ROWS, CA_TQ), 1)
    chunk_start = (col // CHUNK) * CHUNK
    for a in range(CA_WIN // BIAS_ROWS):
        k_in_band = r_loc + (a * BIAS_ROWS) - chunk_start
        in_band = jnp.logical_and(k_in_band >= 0, k_in_band < BAND)
        start = CA_WIN - BIAS_ROWS * (a + 1)
        for h in range(o_ref.shape[0]):
            m = jnp.broadcast_to(ext_ref[h, :, start:start + width], (BIAS_ROWS, width))
            m = pltpu.roll(m, 0, 1, stride=1, stride_axis=0)
            o_ref[h, a * BIAS_ROWS:(a + 1) * BIAS_ROWS, :] = jnp.where(
                in_band, m[:, BIAS_ROWS:BIAS_ROWS + CA_TQ], NEG_INF)


def _ca_bias_table(rel_bias):
    depth, n_heads, n_rel = rel_bias.shape
    rb = rel_bias.reshape(depth * n_heads, n_rel).astype(F32)
    n_lo = CA_TQ - REL_CLIP
    n_hi = BIAS_EXT - n_lo - n_rel
    ext = jnp.concatenate([jnp.broadcast_to(rb[:, :1], (rb.shape[0], n_lo)), rb,
                           jnp.broadcast_to(rb[:, -1:], (rb.shape[0], n_hi))], axis=1)
    return pl.pallas_call(
        _ca_bias_kernel,
        grid=(depth,),
        in_specs=[pl.BlockSpec((n_heads, 1, BIAS_EXT), lambda i: (i, 0, 0))],
        out_specs=pl.BlockSpec((n_heads, CA_WIN, CA_TQ), lambda i: (i, 0, 0)),
        out_shape=jax.ShapeDtypeStruct((depth * n_heads, CA_WIN, CA_TQ), F32),
        compiler_params=_cparams(1),
        name="ca_bias_table",
    )(ext.reshape(depth * n_heads, 1, BIAS_EXT))


def _mlp_kernel(x_ref, osb_ref, oca_ref, mod_ref, g2_ref, wo_hbm, w1_hbm, w2_hbm, o_ref,
                wo_ref, w1_ref, w2_ref, stage_ref, sem_ref, *, layer):
    d, d_ff = w1_ref.shape
    fc = stage_ref.shape[2]
    chunks = [slice(c * fc, (c + 1) * fc) for c in range(d_ff // fc)]

    def body(stager):
        if stager:
            stager.start()
            stager.take(0)
        att = (jnp.dot(osb_ref[...], wo_ref[0:D_GROUP, :], preferred_element_type=F32) +
               jnp.dot(oca_ref[...], wo_ref[D_GROUP:, :], preferred_element_type=F32))
        x1 = x_ref[...] + mod_ref[2:3, :] * att
        h = _modulated_norm(x1, g2_ref[...], mod_ref[3:4, :], mod_ref[4:5, :]).astype(BF16)
        acc = jnp.zeros(x1.shape, F32)
        for c, cols in enumerate(chunks):
            if stager:
                stager.take(1 + 2 * c)
            u = jnp.maximum(jnp.dot(h, w1_ref[:, cols], preferred_element_type=F32), 0.0)
            if stager:
                stager.take(2 + 2 * c)
            acc = acc + jnp.dot((u * u).astype(BF16), w2_ref[cols, :], preferred_element_type=F32)
        o_ref[...] = x1 + mod_ref[5:6, :] * acc

    wo_l, w1_l, w2_l = wo_hbm.at[layer], w1_hbm.at[layer], w2_hbm.at[layer]
    blocks = [(wo_l, _ref_setter(wo_ref, (slice(None), slice(None))))]
    for cols in chunks:
        blocks.append((w1_l.at[:, cols], _ref_setter(w1_ref, (slice(None), cols))))
        blocks.append((w2_l.at[cols, :], _ref_setter(w2_ref, (cols, slice(None)))))
    stager = _WeightStager(blocks, stage_ref, sem_ref)
    pl.when(_first_step())(lambda: body(stager))
    pl.when(jnp.logical_not(_first_step()))(lambda: body(None))


def _outproj_mlp(x, o_sb, o_ca, mod, layer, g2, w_o, w1, w2):
    b, s, d = x.shape
    d_ff = w1.shape[2]
    tm = min(ROW_TILE, s)
    hbm = pl.BlockSpec(memory_space=pl.ANY)
    return pl.pallas_call(
        functools.partial(_mlp_kernel, layer=layer),
        grid=(b, s // tm),
        in_specs=[pl.BlockSpec((None, tm, d), lambda i, j: (i, j, 0)),
                  pl.BlockSpec((None, tm, D_GROUP), lambda i, j: (i, j, 0)),
                  pl.BlockSpec((None, tm, D_GROUP), lambda i, j: (i, j, 0)),
                  pl.BlockSpec((None, None, 6, d), lambda i, j: (layer, i, 0, 0)),
                  _resident((1, d), lambda i, j: (0, 0)),
                  hbm, hbm, hbm],
        out_specs=pl.BlockSpec((None, tm, d), lambda i, j: (i, j, 0)),
        out_shape=jax.ShapeDtypeStruct((b, s, d), F32),
        scratch_shapes=[pltpu.VMEM((d, d), BF16), pltpu.VMEM((d, d_ff), BF16), pltpu.VMEM((d_ff, d), BF16),
                        *_stage_scratch(d, d)],
        compiler_params=_cparams(2),
        name="outproj_mlp",
    )(x, o_sb, o_ca, mod, g2, w_o, w1, w2)


def kernel(x, c, g_norm1, w_in, g_q, g_k, rel_bias, w_o, g_norm2, w1, w2, w_ada, b_ada):
    depth = w_in.shape[0]
    mod = _ada_modulation(c, w_ada, b_ada)
    bias_tab = _ca_bias_table(rel_bias)
    lane_head = jnp.arange(D_GROUP) // HEAD_DIM
    gmat = ((lane_head[:, None] == lane_head[None, :]).astype(F32) * (1.0 / HEAD_DIM)).astype(BF16)
    for l in range(depth):
        gq_t = (jnp.tile(g_q[l], HEADS_PER_GROUP) * QK_SCALE)[None, :]
        gk_t = jnp.tile(g_k[l], HEADS_PER_GROUP)[None, :]
        proj = _inproj(x, mod, l, g_norm1[l][None, :], w_in, gq_t, gk_t, gmat)
        o_sb = _sb_attention(proj)
        o_ca = _ca_attention(proj, bias_tab, l)
        x = _outproj_mlp(x, o_sb, o_ca, mod, l, g_norm2[l][None, :], w_o, w1, w2)
    return x
```

```python
import functools

import jax
import jax.numpy as jnp
from jax import lax
from jax.experimental import pallas as pl
from jax.experimental.pallas import tpu as pltpu

F32 = jnp.float32
BF16 = jnp.bfloat16

HEAD_DIM = 64
LANES = 128
HEADS_PER_GROUP = 8
PAIRS = HEADS_PER_GROUP // 2
D_GROUP = HEADS_PER_GROUP * HEAD_DIM
CHUNK = 64
LEFT_CHUNKS = 8
BAND = (LEFT_CHUNKS + 1) * CHUNK
REL_CLIP = 128
EPS = 1e-6
NEG_INF = -1e30
QK_SCALE = HEAD_DIM ** -0.5
LOG2E = 1.4426950408889634
SOFTPLUS_LINEAR = 64.0

ROW_TILE = 512
INPROJ_ROW_TILE = 1024
SB_TQ = 256
SB_TILES = 4
CA_TQ = 256
CA_TILES = 16
CA_WIN = CA_TQ + LEFT_CHUNKS * CHUNK
SB_DEAD = -88.0
VMEM_LIMIT = 56 * 1024 * 1024


def _cparams(n_axes):
    return pltpu.CompilerParams(dimension_semantics=("arbitrary",) * n_axes,
                                vmem_limit_bytes=VMEM_LIMIT)


def _resident(shape, index_map):
    return pl.BlockSpec(shape, index_map, pipeline_mode=pl.Buffered(1))


def _stage_scratch(rows, cols):
    return pltpu.VMEM((2, rows, cols), F32), pltpu.SemaphoreType.DMA((2,))


class _WeightStager:
    def __init__(self, blocks, stage_ref, sem_ref):
        self.blocks, self.stage_ref, self.sem_ref = blocks, stage_ref, sem_ref

    def _copy(self, k):
        return pltpu.make_async_copy(self.blocks[k][0], self.stage_ref.at[k % 2], self.sem_ref.at[k % 2])

    def start(self):
        self._copy(0).start()

    def take(self, k):
        if k + 1 < len(self.blocks):
            self._copy(k + 1).start()
        self._copy(k).wait()
        self.blocks[k][1](self.stage_ref[k % 2].astype(BF16))


def _ref_setter(ref, idx):
    def store(value):
        ref[idx] = value
    return store


def _first_step():
    return jnp.logical_and(pl.program_id(0) == 0, pl.program_id(1) == 0)


def _ada_kernel(c_ref, w_ref, b_ref, o_ref):
    ca = jax.nn.silu(c_ref[...]).astype(BF16)
    o_ref[...] = jnp.dot(ca, w_ref[...].astype(BF16), preferred_element_type=F32) + b_ref[...]


def _ada_modulation(c, w_ada, b_ada):
    depth, d, n = w_ada.shape
    b = c.shape[0]
    rows = 8
    c_pad = jnp.pad(c, ((0, rows - b), (0, 0)))
    tn = n // 2
    out = pl.pallas_call(
        _ada_kernel,
        grid=(depth, n // tn),
        in_specs=[pl.BlockSpec((rows, d), lambda l, j: (0, 0)),
                  pl.BlockSpec((None, d, tn), lambda l, j: (l, 0, j)),
                  pl.BlockSpec((None, 1, tn), lambda l, j: (l, 0, j))],
        out_specs=pl.BlockSpec((None, rows, tn), lambda l, j: (l, 0, j)),
        out_shape=jax.ShapeDtypeStruct((depth, rows, n), F32),
        compiler_params=_cparams(2),
        name="ada_modulation",
    )(c_pad, w_ada, b_ada.reshape(depth, 1, n))
    return out[:, :b].reshape(depth, b, 6, d)


def _modulated_norm(x, g, shift, scale):
    ms = jnp.mean(x * x, axis=-1, keepdims=True)
    return (x * lax.rsqrt(ms + EPS) * g) * (1.0 + scale) + shift


def _inproj_kernel(x_ref, mod_ref, g1_ref, w_hbm, gq_ref, gk_ref, gmat_ref, o_ref,
                   w_ref, stage_ref, sem_ref, *, layer):
    groups = [slice(c * D_GROUP, (c + 1) * D_GROUP) for c in range(6)]

    def body(stager):
        if stager:
            stager.start()
        h = _modulated_norm(x_ref[...], g1_ref[...], mod_ref[0:1, :], mod_ref[1:2, :]).astype(BF16)
        for c, cols in enumerate(groups):
            if stager:
                stager.take(c)
            y = jnp.dot(h, w_ref[:, cols], preferred_element_type=F32)
            if c == 0:
                y = y * QK_SCALE
            elif c in (3, 4):
                msq = jnp.dot((y * y).astype(BF16), gmat_ref[...], preferred_element_type=F32)
                y = y * lax.rsqrt(msq + EPS) * (gq_ref[...] if c == 3 else gk_ref[...])
            o_ref[:, cols] = y.astype(BF16)

    w_l = w_hbm.at[layer]
    stager = _WeightStager([(w_l.at[:, cols], _ref_setter(w_ref, (slice(None), cols))) for cols in groups],
                           stage_ref, sem_ref)
    pl.when(_first_step())(lambda: body(stager))
    pl.when(jnp.logical_not(_first_step()))(lambda: body(None))


def _inproj(x, mod, layer, g1, w_in, gq_t, gk_t, gmat):
    b, s, d = x.shape
    n = w_in.shape[2]
    tm = min(INPROJ_ROW_TILE, s)
    return pl.pallas_call(
        functools.partial(_inproj_kernel, layer=layer),
        grid=(b, s // tm),
        in_specs=[pl.BlockSpec((None, tm, d), lambda i, j: (i, j, 0)),
                  pl.BlockSpec((None, None, 6, d), lambda i, j: (layer, i, 0, 0)),
                  _resident((1, d), lambda i, j: (0, 0)),
                  pl.BlockSpec(memory_space=pl.ANY),
                  _resident((1, D_GROUP), lambda i, j: (0, 0)),
                  _resident((1, D_GROUP), lambda i, j: (0, 0)),
                  _resident((D_GROUP, D_GROUP), lambda i, j: (0, 0))],
        out_specs=pl.BlockSpec((None, tm, n), lambda i, j: (i, j, 0)),
        out_shape=jax.ShapeDtypeStruct((b, s, n), BF16),
        scratch_shapes=[pltpu.VMEM((d, n), BF16), *_stage_scratch(d, D_GROUP)],
        compiler_params=_cparams(2),
        name="norm1_inproj",
    )(x, mod, g1, w_in, gq_t, gk_t, gmat)


def _head_masked(q):
    lane = lax.broadcasted_iota(jnp.int32, q.shape, 1)
    zero = jnp.zeros_like(q)
    return jnp.where(lane < HEAD_DIM, q, zero), jnp.where(lane >= HEAD_DIM, q, zero)


def _scores_t(k_blk, q_masked):
    return lax.dot_general(k_blk, q_masked, (((1,), (1,)), ((), ())), preferred_element_type=F32)


def _fill_vt(v_ref, vt_ref, blk):
    for i in range(vt_ref.shape[0]):
        vt_ref[i] = v_ref[i * blk:(i + 1) * blk, :].astype(F32).T.astype(BF16)


def _merge_heads_t(out_a, out_b):
    row = lax.broadcasted_iota(jnp.int32, out_a.shape, 0)
    return jnp.where(row < HEAD_DIM, out_a, out_b).T


def _split_bf16(x):
    hi = lax.bitcast_convert_type(lax.bitcast_convert_type(x, jnp.uint32) & jnp.uint32(0xFFFF0000), F32)
    return hi.astype(BF16), (x - hi).astype(BF16)


def _sb_kernel(q_ref, k_ref, v_ref, o_ref, vt_ref, acc_ref):
    tq = vt_ref.shape[2]
    tk = tq
    n_tiles = q_ref.shape[0] // tq
    qi = pl.program_id(2)

    @pl.when(qi == 0)
    def _():
        _fill_vt(v_ref, vt_ref, tk)

    q_heads = [_head_masked(q_ref[t * tq:(t + 1) * tq, :]) for t in range(n_tiles)]
    row = lax.broadcasted_iota(jnp.int32, (tk, tq), 0)
    lane = lax.broadcasted_iota(jnp.int32, (tk, tq), 1)
    tri = (lane >= row).astype(BF16)
    strict = row < lane

    def blocks(work, carries, first):
        depth = max(len(kbs) for kbs in work.values())
        chains = [(t, n, h) for n in range(depth) for t in work if n < len(work[t]) for h in range(2)]
        k_blk = lambda kb: k_ref[pl.ds(pl.multiple_of(kb * tk, tk), tk), :]
        z = {(t, n, h): _scores_t(k_blk(work[t][n]), q_heads[t][h]) for t, n, h in chains}
        incl = {}
        carries = {t: list(carries[t]) for t in work}
        pv = {}

        def suffix_stage(c):
            if first and c[1] == 0:
                z[c] = jnp.where(strict, z[c], NEG_INF)
            sp = jnp.maximum(z[c], jnp.log(1.0 + jnp.exp2(jnp.minimum(z[c], SOFTPLUS_LINEAR) * LOG2E)))
            hi, lo = _split_bf16(sp)
            incl[c] = (jnp.dot(tri, hi, preferred_element_type=F32) +
                       jnp.dot(tri, lo, preferred_element_type=F32))

        def weight_stage(c):
            t, n, h = c
            log_w = z[c] - incl[c]
            if not (first and n == 0):
                log_w = log_w + carries[t][h]
            out = jnp.dot(vt_ref[work[t][n]], jnp.exp(log_w).astype(BF16), preferred_element_type=F32)
            pv[t, h] = out if (t, h) not in pv else pv[t, h] + out
            carries[t][h] = carries[t][h] - incl[c][0:1, :]

        for c in chains:
            suffix_stage(c)
        for c in chains:
            weight_stage(c)
        for t, h in pv:
            if first:
                acc_ref[t, h] = pv[t, h]
            else:
                acc_ref[t, h] += pv[t, h]
        return tuple(tuple(carries[t]) for t in work)

    zero = jnp.zeros((1, tq), F32)
    zeros = {t: (zero, zero) for t in range(n_tiles)}
    base = qi * n_tiles
    full = {t: [base + t, base + t - 1] for t in range(n_tiles)}
    head = dict(full)
    head[0] = [base]
    carries = lax.cond(qi == 0,
                       lambda: blocks(head, zeros, first=True),
                       lambda: blocks(full, zeros, first=True))

    def cond(state):
        kb, ca, cb = state
        alive = jnp.max(jnp.maximum(ca, cb)) > SB_DEAD
        return jnp.logical_and(kb >= 0, alive)

    least_dead = carries[0][0]
    for t in range(n_tiles):
        for h in range(2):
            least_dead = jnp.maximum(least_dead, carries[t][h])

    @pl.when(jnp.max(least_dead) > SB_DEAD)
    def _():
        for t in range(n_tiles):
            def body(state, t=t):
                kb, ca, cb = state
                ((ca, cb),) = blocks({t: [kb]}, {t: (ca, cb)}, first=False)
                return kb - 1, ca, cb

            lax.while_loop(cond, body, (base + t - 2,) + carries[t])

    for t in range(n_tiles):
        o_ref[t * tq:(t + 1) * tq, :] = _merge_heads_t(acc_ref[t, 0], acc_ref[t, 1]).astype(o_ref.dtype)


def _sb_attention(proj):
    b, s, _ = proj.shape
    tq = min(SB_TQ, s)
    step = min(SB_TILES * tq, s)
    return pl.pallas_call(
        _sb_kernel,
        grid=(b, PAIRS, s // step),
        in_specs=[pl.BlockSpec((None, step, LANES), lambda i, p, j: (i, j, p)),
                  pl.BlockSpec((None, s, LANES), lambda i, p, j: (i, 0, PAIRS + p)),
                  pl.BlockSpec((None, s, LANES), lambda i, p, j: (i, 0, 2 * PAIRS + p))],
        out_specs=pl.BlockSpec((None, step, LANES), lambda i, p, j: (i, j, p)),
        out_shape=jax.ShapeDtypeStruct((b, s, D_GROUP), BF16),
        scratch_shapes=[pltpu.VMEM((s // tq, LANES, tq), BF16),
                        pltpu.VMEM((step // tq, 2, LANES, tq), F32)],
        compiler_params=_cparams(3),
        name="stickbreak_attn",
    )(proj, proj, proj)


def _ca_band_rows(i, half, tq):
    start = half * LANES
    stop = (half + 1) * LANES - CHUNK + BAND
    lo = min(max(start, i * tq), (i + 1) * tq) - i * tq
    hi = max(min(stop, (i + 1) * tq), i * tq) - i * tq
    return lo, max(hi, lo)


def _ca_kernel(q_ref, k_ref, v_ref, bias_ref, o_ref, vt_ref, s_ref):
    tq = vt_ref.shape[3]
    n_tiles = q_ref.shape[0] // tq
    n_win = CA_WIN // tq
    qi = pl.program_id(2)

    @pl.when(qi == 0)
    def _():
        row = lax.broadcasted_iota(jnp.int32, (LANES, tq), 0)
        for i in range(vt_ref.shape[1]):
            vt = v_ref[i * tq:(i + 1) * tq, :].astype(F32).T
            vt_ref[0, i] = jnp.where(row < HEAD_DIM, vt, 1.0).astype(BF16)
            vt_ref[1, i] = jnp.where(row >= HEAD_DIM, vt, 1.0).astype(BF16)

    units = [(t, h) for t in range(n_tiles) for h in range(2)]
    halves = [slice(f * LANES, (f + 1) * LANES) for f in range(tq // LANES)]
    q_heads = [_head_masked(q_ref[t * tq:(t + 1) * tq, :]) for t in range(n_tiles)]
    first_blk = [qi * n_tiles + t - (n_win - 1) for t in range(n_tiles)]
    blks = [[jnp.maximum(first_blk[t] + i, 0) for i in range(n_win)] for t in range(n_tiles)]
    k_blk = lambda bi: k_ref[pl.ds(pl.multiple_of(bi * tq, tq), tq), :]

    def attend(sequence_start):
        col_max = {}
        for t, h in units:
            for i in range(n_win):
                raw = _scores_t(k_blk(blks[t][i]), q_heads[t][h])
                for f, lanes in enumerate(halves):
                    lo, hi = _ca_band_rows(i, f, tq)
                    if hi == lo:
                        continue
                    rows = slice(i * tq + lo, i * tq + hi)
                    sc = raw[lo:hi, lanes] + bias_ref[h, rows, lanes]
                    if sequence_start and t + i < n_win - 1:
                        sc = jnp.where(first_blk[t] + i >= 0, sc, NEG_INF)
                    s_ref[t, h, rows, lanes] = sc
                    blk_max = sc.max(axis=0, keepdims=True)
                    col_max[t, h, f] = (jnp.maximum(col_max[t, h, f], blk_max)
                                        if (t, h, f) in col_max else blk_max)
        for t in range(n_tiles):
            outs = []
            for h in range(2):
                acc = None
                for i in range(n_win):
                    cols = []
                    for f, lanes in enumerate(halves):
                        lo, hi = _ca_band_rows(i, f, tq)
                        parts = [jnp.zeros((lo, LANES), BF16)] if lo else []
                        if hi > lo:
                            rows = slice(i * tq + lo, i * tq + hi)
                            parts.append(jnp.exp(s_ref[t, h, rows, lanes] - col_max[t, h, f]).astype(BF16))
                        if hi < tq:
                            parts.append(jnp.zeros((tq - hi, LANES), BF16))
                        cols.append(jnp.concatenate(parts, axis=0) if len(parts) > 1 else parts[0])
                    p = jnp.concatenate(cols, axis=1)
                    out = jnp.dot(vt_ref[h, blks[t][i]], p, preferred_element_type=F32)
                    acc = out if acc is None else acc + out
                ones_row = HEAD_DIM if h == 0 else 0
                outs.append(acc * (1.0 / acc[ones_row:ones_row + 1, :]))
            o_ref[t * tq:(t + 1) * tq, :] = _merge_heads_t(outs[0], outs[1]).astype(o_ref.dtype)

    n_start_steps = -(-(n_win - 1) // n_tiles)
    pl.when(qi < n_start_steps)(lambda: attend(True))
    pl.when(qi >= n_start_steps)(lambda: attend(False))


def _ca_attention(proj, bias_tab, layer):
    b, s, _ = proj.shape
    tq = CA_TQ
    step = min(CA_TILES * tq, s)
    base = 3 * PAIRS
    return pl.pallas_call(
        _ca_kernel,
        grid=(b, PAIRS, s // step),
        in_specs=[pl.BlockSpec((None, step, LANES), lambda i, p, j: (i, j, base + p)),
                  pl.BlockSpec((None, s, LANES), lambda i, p, j: (i, 0, base + PAIRS + p)),
                  pl.BlockSpec((None, s, LANES), lambda i, p, j: (i, 0, base + 2 * PAIRS + p)),
                  pl.BlockSpec((2, CA_WIN, tq), lambda i, p, j: (layer * PAIRS + p, 0, 0))],
        out_specs=pl.BlockSpec((None, step, LANES), lambda i, p, j: (i, j, p)),
        out_shape=jax.ShapeDtypeStruct((b, s, D_GROUP), BF16),
        scratch_shapes=[pltpu.VMEM((2, s // tq, LANES, tq), BF16),
                        pltpu.VMEM((step // tq, 2, CA_WIN, tq), F32)],
        compiler_params=_cparams(3),
        name="chunkrel_attn",
    )(proj, proj, proj, bias_tab)


BIAS_ROWS = 128
BIAS_EXT = CA_WIN + CA_TQ


def _ca_bias_kernel(ext_ref, o_ref):
    width = CA_TQ + BIAS_ROWS
    r_loc = lax.broadcasted_iota(jnp.int32, (BIAS_ROWS, CA_TQ), 0)
    col = lax.broadcasted_iota(jnp.int32, (BIAS_ROWS, CA_TQ), 1)
    chunk_start = (col // CHUNK) * CHUNK
    for a in range(CA_WIN // BIAS_ROWS):
        k_in_band = r_loc + (a * BIAS_ROWS) - chunk_start
        in_band = jnp.logical_and(k_in_band >= 0, k_in_band < BAND)
        start = CA_WIN - BIAS_ROWS * (a + 1)
        for h in range(o_ref.shape[0]):
            m = jnp.broadcast_to(ext_ref[h, :, start:start + width], (BIAS_ROWS, width))
            m = pltpu.roll(m, 0, 1, stride=1, stride_axis=0)
            o_ref[h, a * BIAS_ROWS:(a + 1) * BIAS_ROWS, :] = jnp.where(
                in_band, m[:, BIAS_ROWS:BIAS_ROWS + CA_TQ], NEG_INF)


def _ca_bias_table(rel_bias):
    depth, n_heads, n_rel = rel_bias.shape
    rb = rel_bias.reshape(depth * n_heads, n_rel).astype(F32)
    n_lo = CA_TQ - REL_CLIP
    n_hi = BIAS_EXT - n_lo - n_rel
    ext = jnp.concatenate([jnp.broadcast_to(rb[:, :1], (rb.shape[0], n_lo)), rb,
                           jnp.broadcast_to(rb[:, -1:], (rb.shape[0], n_hi))], axis=1)
    return pl.pallas_call(
        _ca_bias_kernel,
        grid=(depth,),
        in_specs=[pl.BlockSpec((n_heads, 1, BIAS_EXT), lambda i: (i, 0, 0))],
        out_specs=pl.BlockSpec((n_heads, CA_WIN, CA_TQ), lambda i: (i, 0, 0)),
        out_shape=jax.ShapeDtypeStruct((depth * n_heads, CA_WIN, CA_TQ), F32),
        compiler_params=_cparams(1),
        name="ca_bias_table",
    )(ext.reshape(depth * n_heads, 1, BIAS_EXT))


def _mlp_kernel(x_ref, osb_ref, oca_ref, mod_ref, g2_ref, wo_hbm, w1_hbm, w2_hbm, o_ref,
                wo_ref, w1_ref, w2_ref, stage_ref, sem_ref, *, layer):
    d, d_ff = w1_ref.shape
    fc = stage_ref.shape[2]
    chunks = [slice(c * fc, (c + 1) * fc) for c in range(d_ff // fc)]

    def body(stager):
        if stager:
            stager.start()
            stager.take(0)
        att = (jnp.dot(osb_ref[...], wo_ref[0:D_GROUP, :], preferred_element_type=F32) +
               jnp.dot(oca_ref[...], wo_ref[D_GROUP:, :], preferred_element_type=F32))
        x1 = x_ref[...] + mod_ref[2:3, :] * att
        h = _modulated_norm(x1, g2_ref[...], mod_ref[3:4, :], mod_ref[4:5, :]).astype(BF16)
        acc = jnp.zeros(x1.shape, F32)
        for c, cols in enumerate(chunks):
            if stager:
                stager.take(1 + 2 * c)
            u = jnp.maximum(jnp.dot(h, w1_ref[:, cols], preferred_element_type=F32), 0.0)
            if stager:
                stager.take(2 + 2 * c)
            acc = acc + jnp.dot((u * u).astype(BF16), w2_ref[cols, :], preferred_element_type=F32)
        o_ref[...] = x1 + mod_ref[5:6, :] * acc

    wo_l, w1_l, w2_l = wo_hbm.at[layer], w1_hbm.at[layer], w2_hbm.at[layer]
    blocks = [(wo_l, _ref_setter(wo_ref, (slice(None), slice(None))))]
    for cols in chunks:
        blocks.append((w1_l.at[:, cols], _ref_setter(w1_ref, (slice(None), cols))))
        blocks.append((w2_l.at[cols, :], _ref_setter(w2_ref, (cols, slice(None)))))
    stager = _WeightStager(blocks, stage_ref, sem_ref)
    pl.when(_first_step())(lambda: body(stager))
    pl.when(jnp.logical_not(_first_step()))(lambda: body(None))


def _outproj_mlp(x, o_sb, o_ca, mod, layer, g2, w_o, w1, w2):
    b, s, d = x.shape
    d_ff = w1.shape[2]
    tm = min(ROW_TILE, s)
    hbm = pl.BlockSpec(memory_space=pl.ANY)
    return pl.pallas_call(
        functools.partial(_mlp_kernel, layer=layer),
        grid=(b, s // tm),
        in_specs=[pl.BlockSpec((None, tm, d), lambda i, j: (i, j, 0)),
                  pl.BlockSpec((None, tm, D_GROUP), lambda i, j: (i, j, 0)),
                  pl.BlockSpec((None, tm, D_GROUP), lambda i, j: (i, j, 0)),
                  pl.BlockSpec((None, None, 6, d), lambda i, j: (layer, i, 0, 0)),
                  _resident((1, d), lambda i, j: (0, 0)),
                  hbm, hbm, hbm],
        out_specs=pl.BlockSpec((None, tm, d), lambda i, j: (i, j, 0)),
        out_shape=jax.ShapeDtypeStruct((b, s, d), F32),
        scratch_shapes=[pltpu.VMEM((d, d), BF16), pltpu.VMEM((d, d_ff), BF16), pltpu.VMEM((d_ff, d), BF16),
                        *_stage_scratch(d, d)],
        compiler_params=_cparams(2),
        name="outproj_mlp",
    )(x, o_sb, o_ca, mod, g2, w_o, w1, w2)


def kernel(x, c, g_norm1, w_in, g_q, g_k, rel_bias, w_o, g_norm2, w1, w2, w_ada, b_ada):
    depth = w_in.shape[0]
    mod = _ada_modulation(c, w_ada, b_ada)
    bias_tab = _ca_bias_table(rel_bias)
    lane_head = jnp.arange(D_GROUP) // HEAD_DIM
    gmat = ((lane_head[:, None] == lane_head[None, :]).astype(F32) * (1.0 / HEAD_DIM)).astype(BF16)
    for l in range(depth):
        gq_t = (jnp.tile(g_q[l], HEADS_PER_GROUP) * QK_SCALE)[None, :]
        gk_t = jnp.tile(g_k[l], HEADS_PER_GROUP)[None, :]
        proj = _inproj(x, mod, l, g_norm1[l][None, :], w_in, gq_t, gk_t, gmat)
        o_sb = _sb_attention(proj)
        o_ca = _ca_attention(proj, bias_tab, l)
        x = _outproj_mlp(x, o_sb, o_ca, mod, l, g_norm2[l][None, :], w_o, w1, w2)
    return x
```

```python
import functools

import jax
import jax.numpy as jnp
from jax import lax
from jax.experimental import pallas as pl
from jax.experimental.pallas import tpu as pltpu

F32 = jnp.float32
BF16 = jnp.bfloat16

HEAD_DIM = 64
LANES = 128
HEADS_PER_GROUP = 8
PAIRS = HEADS_PER_GROUP // 2
D_GROUP = HEADS_PER_GROUP * HEAD_DIM
CHUNK = 64
LEFT_CHUNKS = 8
BAND = (LEFT_CHUNKS + 1) * CHUNK
REL_CLIP = 128
EPS = 1e-6
NEG_INF = -1e30
QK_SCALE = HEAD_DIM ** -0.5
LOG2E = 1.4426950408889634
SOFTPLUS_LINEAR = 64.0

ROW_TILE = 512
INPROJ_ROW_TILE = 1024
SB_TQ = 256
SB_TILES = 2
CA_TQ = 256
CA_TILES = 16
CA_WIN = CA_TQ + LEFT_CHUNKS * CHUNK
SB_DEAD = -88.0
VMEM_LIMIT = 56 * 1024 * 1024


def _cparams(n_axes):
    return pltpu.CompilerParams(dimension_semantics=("arbitrary",) * n_axes,
                                vmem_limit_bytes=VMEM_LIMIT)


def _resident(shape, index_map):
    return pl.BlockSpec(shape, index_map, pipeline_mode=pl.Buffered(1))


def _stage_scratch(rows, cols):
    return pltpu.VMEM((2, rows, cols), F32), pltpu.SemaphoreType.DMA((2,))


class _WeightStager:
    def __init__(self, blocks, stage_ref, sem_ref):
        self.blocks, self.stage_ref, self.sem_ref = blocks, stage_ref, sem_ref

    def _copy(self, k):
        return pltpu.make_async_copy(self.blocks[k][0], self.stage_ref.at[k % 2], self.sem_ref.at[k % 2])

    def start(self):
        self._copy(0).start()

    def take(self, k):
        if k + 1 < len(self.blocks):
            self._copy(k + 1).start()
        self._copy(k).wait()
        self.blocks[k][1](self.stage_ref[k % 2].astype(BF16))


def _ref_setter(ref, idx):
    def store(value):
        ref[idx] = value
    return store


def _first_step():
    return jnp.logical_and(pl.program_id(0) == 0, pl.program_id(1) == 0)


def _ada_kernel(c_ref, w_ref, b_ref, o_ref):
    ca = jax.nn.silu(c_ref[...]).astype(BF16)
    o_ref[...] = jnp.dot(ca, w_ref[...].astype(BF16), preferred_element_type=F32) + b_ref[...]


def _ada_modulation(c, w_ada, b_ada):
    depth, d, n = w_ada.shape
    b = c.shape[0]
    rows = 8
    c_pad = jnp.pad(c, ((0, rows - b), (0, 0)))
    tn = n // 2
    out = pl.pallas_call(
        _ada_kernel,
        grid=(depth, n // tn),
        in_specs=[pl.BlockSpec((rows, d), lambda l, j: (0, 0)),
                  pl.BlockSpec((None, d, tn), lambda l, j: (l, 0, j)),
                  pl.BlockSpec((None, 1, tn), lambda l, j: (l, 0, j))],
        out_specs=pl.BlockSpec((None, rows, tn), lambda l, j: (l, 0, j)),
        out_shape=jax.ShapeDtypeStruct((depth, rows, n), F32),
        compiler_params=_cparams(2),
        name="ada_modulation",
    )(c_pad, w_ada, b_ada.reshape(depth, 1, n))
    return out[:, :b].reshape(depth, b, 6, d)


def _modulated_norm(x, g, shift, scale):
    ms = jnp.mean(x * x, axis=-1, keepdims=True)
    return (x * lax.rsqrt(ms + EPS) * g) * (1.0 + scale) + shift


def _inproj_kernel(x_ref, mod_ref, g1_ref, w_hbm, gq_ref, gk_ref, gmat_ref, o_ref,
                   w_ref, stage_ref, sem_ref, *, layer):
    groups = [slice(c * D_GROUP, (c + 1) * D_GROUP) for c in range(6)]

    def body(stager):
        if stager:
            stager.start()
        h = _modulated_norm(x_ref[...], g1_ref[...], mod_ref[0:1, :], mod_ref[1:2, :]).astype(BF16)
        for c, cols in enumerate(groups):
            if stager:
                stager.take(c)
            y = jnp.dot(h, w_ref[:, cols], preferred_element_type=F32)
            if c == 0:
                y = y * QK_SCALE
            elif c in (3, 4):
                msq = jnp.dot((y * y).astype(BF16), gmat_ref[...], preferred_element_type=F32)
                y = y * lax.rsqrt(msq + EPS) * (gq_ref[...] if c == 3 else gk_ref[...])
            o_ref[:, cols] = y.astype(BF16)

    w_l = w_hbm.at[layer]
    stager = _WeightStager([(w_l.at[:, cols], _ref_setter(w_ref, (slice(None), cols))) for cols in groups],
                           stage_ref, sem_ref)
    pl.when(_first_step())(lambda: body(stager))
    pl.when(jnp.logical_not(_first_step()))(lambda: body(None))


def _inproj(x, mod, layer, g1, w_in, gq_t, gk_t, gmat):
    b, s, d = x.shape
    n = w_in.shape[2]
    tm = min(INPROJ_ROW_TILE, s)
    return pl.pallas_call(
        functools.partial(_inproj_kernel, layer=layer),
        grid=(b, s // tm),
        in_specs=[pl.BlockSpec((None, tm, d), lambda i, j: (i, j, 0)),
                  pl.BlockSpec((None, None, 6, d), lambda i, j: (layer, i, 0, 0)),
                  _resident((1, d), lambda i, j: (0, 0)),
                  pl.BlockSpec(memory_space=pl.ANY),
                  _resident((1, D_GROUP), lambda i, j: (0, 0)),
                  _resident((1, D_GROUP), lambda i, j: (0, 0)),
                  _resident((D_GROUP, D_GROUP), lambda i, j: (0, 0))],
        out_specs=pl.BlockSpec((None, tm, n), lambda i, j: (i, j, 0)),
        out_shape=jax.ShapeDtypeStruct((b, s, n), BF16),
        scratch_shapes=[pltpu.VMEM((d, n), BF16), *_stage_scratch(d, D_GROUP)],
        compiler_params=_cparams(2),
        name="norm1_inproj",
    )(x, mod, g1, w_in, gq_t, gk_t, gmat)


def _head_masked(q):
    lane = lax.broadcasted_iota(jnp.int32, q.shape, 1)
    zero = jnp.zeros_like(q)
    return jnp.where(lane < HEAD_DIM, q, zero), jnp.where(lane >= HEAD_DIM, q, zero)


def _scores_t(k_blk, q_masked):
    return lax.dot_general(k_blk, q_masked, (((1,), (1,)), ((), ())), preferred_element_type=F32)


def _fill_vt(v_ref, vt_ref, blk):
    for i in range(vt_ref.shape[0]):
        vt_ref[i] = v_ref[i * blk:(i + 1) * blk, :].astype(F32).T.astype(BF16)


def _merge_heads_t(out_a, out_b):
    row = lax.broadcasted_iota(jnp.int32, out_a.shape, 0)
    return jnp.where(row < HEAD_DIM, out_a, out_b).T


def _split_bf16(x):
    hi = lax.bitcast_convert_type(lax.bitcast_convert_type(x, jnp.uint32) & jnp.uint32(0xFFFF0000), F32)
    return hi.astype(BF16), (x - hi).astype(BF16)


def _sb_kernel(q_ref, k_ref, v_ref, o_ref, vt_ref, acc_ref):
    tq = vt_ref.shape[2]
    tk = tq
    n_tiles = q_ref.shape[0] // tq
    qi = pl.program_id(2)

    @pl.when(qi == 0)
    def _():
        _fill_vt(v_ref, vt_ref, tk)

    q_heads = [_head_masked(q_ref[t * tq:(t + 1) * tq, :]) for t in range(n_tiles)]
    row = lax.broadcasted_iota(jnp.int32, (tk, tq), 0)
    lane = lax.broadcasted_iota(jnp.int32, (tk, tq), 1)
    tri = (lane >= row).astype(BF16)
    strict = row < lane

    def blocks(work, carries, first):
        depth = max(len(kbs) for kbs in work.values())
        chains = [(t, n, h) for n in range(depth) for t in work if n < len(work[t]) for h in range(2)]
        k_blk = lambda kb: k_ref[pl.ds(pl.multiple_of(kb * tk, tk), tk), :]
        z = {(t, n, h): _scores_t(k_blk(work[t][n]), q_heads[t][h]) for t, n, h in chains}
        incl = {}
        carries = {t: list(carries[t]) for t in work}
        pv = {}

        def suffix_stage(c):
            if first and c[1] == 0:
                z[c] = jnp.where(strict, z[c], NEG_INF)
            sp = jnp.maximum(z[c], jnp.log(1.0 + jnp.exp2(jnp.minimum(z[c], SOFTPLUS_LINEAR) * LOG2E)))
            hi, lo = _split_bf16(sp)
            incl[c] = (jnp.dot(tri, hi, preferred_element_type=F32) +
                       jnp.dot(tri, lo, preferred_element_type=F32))

        def weight_stage(c):
            t, n, h = c
            log_w = z[c] - incl[c]
            if not (first and n == 0):
                log_w = log_w + carries[t][h]
            out = jnp.dot(vt_ref[work[t][n]], jnp.exp(log_w).astype(BF16), preferred_element_type=F32)
            pv[t, h] = out if (t, h) not in pv else pv[t, h] + out
            carries[t][h] = carries[t][h] - incl[c][0:1, :]

        for c in chains:
            suffix_stage(c)
        for c in chains:
            weight_stage(c)
        for t, h in pv:
            if first:
                acc_ref[t, h] = pv[t, h]
            else:
                acc_ref[t, h] += pv[t, h]
        return tuple(tuple(carries[t]) for t in work)

    zero = jnp.zeros((1, tq), F32)
    zeros = {t: (zero, zero) for t in range(n_tiles)}
    base = qi * n_tiles
    full = {t: [base + t, base + t - 1] for t in range(n_tiles)}
    head = dict(full)
    head[0] = [base]
    carries = lax.cond(qi == 0,
                       lambda: blocks(head, zeros, first=True),
                       lambda: blocks(full, zeros, first=True))

    def cond(state):
        kb, ca, cb = state
        alive = jnp.max(jnp.maximum(ca, cb)) > SB_DEAD
        return jnp.logical_and(kb >= 0, alive)

    least_dead = carries[0][0]
    for t in range(n_tiles):
        for h in range(2):
            least_dead = jnp.maximum(least_dead, carries[t][h])

    @pl.when(jnp.max(least_dead) > SB_DEAD)
    def _():
        for t in range(n_tiles):
            def body(state, t=t):
                kb, ca, cb = state
                ((ca, cb),) = blocks({t: [kb]}, {t: (ca, cb)}, first=False)
                return kb - 1, ca, cb

            lax.while_loop(cond, body, (base + t - 2,) + carries[t])

    for t in range(n_tiles):
        o_ref[t * tq:(t + 1) * tq, :] = _merge_heads_t(acc_ref[t, 0], acc_ref[t, 1]).astype(o_ref.dtype)


def _sb_attention(proj):
    b, s, _ = proj.shape
    tq = min(SB_TQ, s)
    step = min(SB_TILES * tq, s)
    return pl.pallas_call(
        _sb_kernel,
        grid=(b, PAIRS, s // step),
        in_specs=[pl.BlockSpec((None, step, LANES), lambda i, p, j: (i, j, p)),
                  pl.BlockSpec((None, s, LANES), lambda i, p, j: (i, 0, PAIRS + p)),
                  pl.BlockSpec((None, s, LANES), lambda i, p, j: (i, 0, 2 * PAIRS + p))],
        out_specs=pl.BlockSpec((None, step, LANES), lambda i, p, j: (i, j, p)),
        out_shape=jax.ShapeDtypeStruct((b, s, D_GROUP), BF16),
        scratch_shapes=[pltpu.VMEM((s // tq, LANES, tq), BF16),
                        pltpu.VMEM((step // tq, 2, LANES, tq), F32)],
        compiler_params=_cparams(3),
        name="stickbreak_attn",
    )(proj, proj, proj)


def _ca_band_rows(i, half, tq):
    start = half * LANES
    stop = (half + 1) * LANES - CHUNK + BAND
    lo = min(max(start, i * tq), (i + 1) * tq) - i * tq
    hi = max(min(stop, (i + 1) * tq), i * tq) - i * tq
    return lo, max(hi, lo)


def _ca_kernel(q_ref, k_ref, v_ref, bias_ref, o_ref, vt_ref, s_ref):
    tq = vt_ref.shape[3]
    n_tiles = q_ref.shape[0] // tq
    n_win = CA_WIN // tq
    qi = pl.program_id(2)

    @pl.when(qi == 0)
    def _():
        row = lax.broadcasted_iota(jnp.int32, (LANES, tq), 0)
        for i in range(vt_ref.shape[1]):
            vt = v_ref[i * tq:(i + 1) * tq, :].astype(F32).T
            vt_ref[0, i] = jnp.where(row < HEAD_DIM, vt, 1.0).astype(BF16)
            vt_ref[1, i] = jnp.where(row >= HEAD_DIM, vt, 1.0).astype(BF16)

    units = [(t, h) for t in range(n_tiles) for h in range(2)]
    halves = [slice(f * LANES, (f + 1) * LANES) for f in range(tq // LANES)]
    q_heads = [_head_masked(q_ref[t * tq:(t + 1) * tq, :]) for t in range(n_tiles)]
    first_blk = [qi * n_tiles + t - (n_win - 1) for t in range(n_tiles)]
    blks = [[jnp.maximum(first_blk[t] + i, 0) for i in range(n_win)] for t in range(n_tiles)]
    k_blk = lambda bi: k_ref[pl.ds(pl.multiple_of(bi * tq, tq), tq), :]

    def attend(sequence_start):
        col_max = {}
        for t, h in units:
            for i in range(n_win):
                raw = _scores_t(k_blk(blks[t][i]), q_heads[t][h])
                for f, lanes in enumerate(halves):
                    lo, hi = _ca_band_rows(i, f, tq)
                    if hi == lo:
                        continue
                    rows = slice(i * tq + lo, i * tq + hi)
                    sc = raw[lo:hi, lanes] + bias_ref[h, rows, lanes]
                    if sequence_start and t + i < n_win - 1:
                        sc = jnp.where(first_blk[t] + i >= 0, sc, NEG_INF)
                    s_ref[t, h, rows, lanes] = sc
                    blk_max = sc.max(axis=0, keepdims=True)
                    col_max[t, h, f] = (jnp.maximum(col_max[t, h, f], blk_max)
                                        if (t, h, f) in col_max else blk_max)
        for t in range(n_tiles):
            outs = []
            for h in range(2):
                acc = None
                for i in range(n_win):
                    cols = []
                    for f, lanes in enumerate(halves):
                        lo, hi = _ca_band_rows(i, f, tq)
                        parts = [jnp.zeros((lo, LANES), BF16)] if lo else []
                        if hi > lo:
                            rows = slice(i * tq + lo, i * tq + hi)
                            parts.append(jnp.exp(s_ref[t, h, rows, lanes] - col_max[t, h, f]).astype(BF16))
                        if hi < tq:
                            parts.append(jnp.zeros((tq - hi, LANES), BF16))
                        cols.append(jnp.concatenate(parts, axis=0) if len(parts) > 1 else parts[0])
                    p = jnp.concatenate(cols, axis=1)
                    out = jnp.dot(vt_ref[h, blks[t][i]], p, preferred_element_type=F32)
                    acc = out if acc is None else acc + out
                ones_row = HEAD_DIM if h == 0 else 0
                outs.append(acc * (1.0 / acc[ones_row:ones_row + 1, :]))
            o_ref[t * tq:(t + 1) * tq, :] = _merge_heads_t(outs[0], outs[1]).astype(o_ref.dtype)

    n_start_steps = -(-(n_win - 1) // n_tiles)
    pl.when(qi < n_start_steps)(lambda: attend(True))
    pl.when(qi >= n_start_steps)(lambda: attend(False))


def _ca_attention(proj, bias_tab, layer):
    b, s, _ = proj.shape
    tq = CA_TQ
    step = min(CA_TILES * tq, s)
    base = 3 * PAIRS
    return pl.pallas_call(
        _ca_kernel,
        grid=(b, PAIRS, s // step),
        in_specs=[pl.BlockSpec((None, step, LANES), lambda i, p, j: (i, j, base + p)),
                  pl.BlockSpec((None, s, LANES), lambda i, p, j: (i, 0, base + PAIRS + p)),
                  pl.BlockSpec((None, s, LANES), lambda i, p, j: (i, 0, base + 2 * PAIRS + p)),
                  pl.BlockSpec((2, CA_WIN, tq), lambda i, p, j: (layer * PAIRS + p, 0, 0))],
        out_specs=pl.BlockSpec((None, step, LANES), lambda i, p, j: (i, j, p)),
        out_shape=jax.ShapeDtypeStruct((b, s, D_GROUP), BF16),
        scratch_shapes=[pltpu.VMEM((2, s // tq, LANES, tq), BF16),
                        pltpu.VMEM((step // tq, 2, CA_WIN, tq), F32)],
        compiler_params=_cparams(3),
        name="chunkrel_attn",
    )(proj, proj, proj, bias_tab)


BIAS_ROWS = 128
BIAS_EXT = CA_WIN + CA_TQ


def _ca_bias_kernel(ext_ref, o_ref):
    width = CA_TQ + BIAS_ROWS
    r_loc = lax.broadcasted_iota(jnp.int32, (BIAS_ROWS, CA_TQ), 0)
    col = lax.broadcasted_iota(jnp.int32, (BIAS_ROWS, CA_TQ), 1)
    chunk_start = (col // CHUNK) * CHUNK
    for a in range(CA_WIN // BIAS_ROWS):
        k_in_band = r_loc + (a * BIAS_ROWS) - chunk_start
        in_band = jnp.logical_and(k_in_band >= 0, k_in_band < BAND)
        start = CA_WIN - BIAS_ROWS * (a + 1)
        for h in range(o_ref.shape[0]):
            m = jnp.broadcast_to(ext_ref[h, :, start:start + width], (BIAS_ROWS, width))
            m = pltpu.roll(m, 0, 1, stride=1, stride_axis=0)
            o_ref[h, a * BIAS_ROWS:(a + 1) * BIAS_ROWS, :] = jnp.where(
                in_band, m[:, BIAS_ROWS:BIAS_ROWS + CA_TQ], NEG_INF)


def _ca_bias_table(rel_bias):
    depth, n_heads, n_rel = rel_bias.shape
    rb = rel_bias.reshape(depth * n_heads, n_rel).astype(F32)
    n_lo = CA_TQ - REL_CLIP
    n_hi = BIAS_EXT - n_lo - n_rel
    ext = jnp.concatenate([jnp.broadcast_to(rb[:, :1], (rb.shape[0], n_lo)), rb,
                           jnp.broadcast_to(rb[:, -1:], (rb.shape[0], n_hi))], axis=1)
    return pl.pallas_call(
        _ca_bias_kernel,
        grid=(depth,),
        in_specs=[pl.BlockSpec((n_heads, 1, BIAS_EXT), lambda i: (i, 0, 0))],
        out_specs=pl.BlockSpec((n_heads, CA_WIN, CA_TQ), lambda i: (i, 0, 0)),
        out_shape=jax.ShapeDtypeStruct((depth * n_heads, CA_WIN, CA_TQ), F32),
        compiler_params=_cparams(1),
        name="ca_bias_table",
    )(ext.reshape(depth * n_heads, 1, BIAS_EXT))


def _mlp_kernel(x_ref, osb_ref, oca_ref, mod_ref, g2_ref, wo_hbm, w1_hbm, w2_hbm, o_ref,
                wo_ref, w1_ref, w2_ref, stage_ref, sem_ref, *, layer):
    d, d_ff = w1_ref.shape
    fc = stage_ref.shape[2]
    chunks = [slice(c * fc, (c + 1) * fc) for c in range(d_ff // fc)]

    def body(stager):
        if stager:
            stager.start()
            stager.take(0)
        att = (jnp.dot(osb_ref[...], wo_ref[0:D_GROUP, :], preferred_element_type=F32) +
               jnp.dot(oca_ref[...], wo_ref[D_GROUP:, :], preferred_element_type=F32))
        x1 = x_ref[...] + mod_ref[2:3, :] * att
        h = _modulated_norm(x1, g2_ref[...], mod_ref[3:4, :], mod_ref[4:5, :]).astype(BF16)
        acc = jnp.zeros(x1.shape, F32)
        for c, cols in enumerate(chunks):
            if stager:
                stager.take(1 + 2 * c)
            u = jnp.maximum(jnp.dot(h, w1_ref[:, cols], preferred_element_type=F32), 0.0)
            if stager:
                stager.take(2 + 2 * c)
            acc = acc + jnp.dot((u * u).astype(BF16), w2_ref[cols, :], preferred_element_type=F32)
        o_ref[...] = x1 + mod_ref[5:6, :] * acc

    wo_l, w1_l, w2_l = wo_hbm.at[layer], w1_hbm.at[layer], w2_hbm.at[layer]
    blocks = [(wo_l, _ref_setter(wo_ref, (slice(None), slice(None))))]
    for cols in chunks:
        blocks.append((w1_l.at[:, cols], _ref_setter(w1_ref, (slice(None), cols))))
        blocks.append((w2_l.at[cols, :], _ref_setter(w2_ref, (cols, slice(None)))))
    stager = _WeightStager(blocks, stage_ref, sem_ref)
    pl.when(_first_step())(lambda: body(stager))
    pl.when(jnp.logical_not(_first_step()))(lambda: body(None))


def _outproj_mlp(x, o_sb, o_ca, mod, layer, g2, w_o, w1, w2):
    b, s, d = x.shape
    d_ff = w1.shape[2]
    tm = min(ROW_TILE, s)
    hbm = pl.BlockSpec(memory_space=pl.ANY)
    return pl.pallas_call(
        functools.partial(_mlp_kernel, layer=layer),
        grid=(b, s // tm),
        in_specs=[pl.BlockSpec((None, tm, d), lambda i, j: (i, j, 0)),
                  pl.BlockSpec((None, tm, D_GROUP), lambda i, j: (i, j, 0)),
                  pl.BlockSpec((None, tm, D_GROUP), lambda i, j: (i, j, 0)),
                  pl.BlockSpec((None, None, 6, d), lambda i, j: (layer, i, 0, 0)),
                  _resident((1, d), lambda i, j: (0, 0)),
                  hbm, hbm, hbm],
        out_specs=pl.BlockSpec((None, tm, d), lambda i, j: (i, j, 0)),
        out_shape=jax.ShapeDtypeStruct((b, s, d), F32),
        scratch_shapes=[pltpu.VMEM((d, d), BF16), pltpu.VMEM((d, d_ff), BF16), pltpu.VMEM((d_ff, d), BF16),
                        *_stage_scratch(d, d)],
        compiler_params=_cparams(2),
        name="outproj_mlp",
    )(x, o_sb, o_ca, mod, g2, w_o, w1, w2)


def kernel(x, c, g_norm1, w_in, g_q, g_k, rel_bias, w_o, g_norm2, w1, w2, w_ada, b_ada):
    depth = w_in.shape[0]
    mod = _ada_modulation(c, w_ada, b_ada)
    bias_tab = _ca_bias_table(rel_bias)
    lane_head = jnp.arange(D_GROUP) // HEAD_DIM
    gmat = ((lane_head[:, None] == lane_head[None, :]).astype(F32) * (1.0 / HEAD_DIM)).astype(BF16)
    for l in range(depth):
        gq_t = (jnp.tile(g_q[l], HEADS_PER_GROUP) * QK_SCALE)[None, :]
        gk_t = jnp.tile(g_k[l], HEADS_PER_GROUP)[None, :]
        proj = _inproj(x, mod, l, g_norm1[l][None, :], w_in, gq_t, gk_t, gmat)
        o_sb = _sb_attention(proj)
        o_ca = _ca_attention(proj, bias_tab, l)
        x = _outproj_mlp(x, o_sb, o_ca, mod, l, g_norm2[l][None, :], w_o, w1, w2)
    return x
```

```python
import functools

import jax
import jax.numpy as jnp
from jax import lax
from jax.experimental import pallas as pl
from jax.experimental.pallas import tpu as pltpu

F32 = jnp.float32
BF16 = jnp.bfloat16

HEAD_DIM = 64
LANES = 128
HEADS_PER_GROUP = 8
PAIRS = HEADS_PER_GROUP // 2
D_GROUP = HEADS_PER_GROUP * HEAD_DIM
CHUNK = 64
LEFT_CHUNKS = 8
BAND = (LEFT_CHUNKS + 1) * CHUNK
REL_CLIP = 128
EPS = 1e-6
NEG_INF = -1e30
QK_SCALE = HEAD_DIM ** -0.5
LOG2E = 1.4426950408889634
SOFTPLUS_LINEAR = 64.0

ROW_TILE = 512
INPROJ_ROW_TILE = 1024
SB_TQ = 256
SB_TILES = 4
CA_TQ = 256
CA_TILES = 16
CA_WIN = CA_TQ + LEFT_CHUNKS * CHUNK
SB_DEAD = -88.0
VMEM_LIMIT = 56 * 1024 * 1024


def _cparams(n_axes):
    return pltpu.CompilerParams(dimension_semantics=("arbitrary",) * n_axes,
                                vmem_limit_bytes=VMEM_LIMIT)


def _resident(shape, index_map):
    return pl.BlockSpec(shape, index_map, pipeline_mode=pl.Buffered(1))


def _stage_scratch(rows, cols):
    return pltpu.VMEM((2, rows, cols), F32), pltpu.SemaphoreType.DMA((2,))


class _WeightStager:
    def __init__(self, blocks, stage_ref, sem_ref):
        self.blocks, self.stage_ref, self.sem_ref = blocks, stage_ref, sem_ref

    def _copy(self, k):
        return pltpu.make_async_copy(self.blocks[k][0], self.stage_ref.at[k % 2], self.sem_ref.at[k % 2])

    def start(self):
        self._copy(0).start()

    def take(self, k):
        if k + 1 < len(self.blocks):
            self._copy(k + 1).start()
        self._copy(k).wait()
        self.blocks[k][1](self.stage_ref[k % 2].astype(BF16))


def _ref_setter(ref, idx):
    def store(value):
        ref[idx] = value
    return store


def _first_step():
    return jnp.logical_and(pl.program_id(0) == 0, pl.program_id(1) == 0)


def _ada_kernel(c_ref, w_ref, b_ref, o_ref):
    ca = jax.nn.silu(c_ref[...]).astype(BF16)
    o_ref[...] = jnp.dot(ca, w_ref[...].astype(BF16), preferred_element_type=F32) + b_ref[...]


def _ada_modulation(c, w_ada, b_ada):
    depth, d, n = w_ada.shape
    b = c.shape[0]
    rows = 8
    c_pad = jnp.pad(c, ((0, rows - b), (0, 0)))
    tn = n // 2
    out = pl.pallas_call(
        _ada_kernel,
        grid=(depth, n // tn),
        in_specs=[pl.BlockSpec((rows, d), lambda l, j: (0, 0)),
                  pl.BlockSpec((None, d, tn), lambda l, j: (l, 0, j)),
                  pl.BlockSpec((None, 1, tn), lambda l, j: (l, 0, j))],
        out_specs=pl.BlockSpec((None, rows, tn), lambda l, j: (l, 0, j)),
        out_shape=jax.ShapeDtypeStruct((depth, rows, n), F32),
        compiler_params=_cparams(2),
        name="ada_modulation",
    )(c_pad, w_ada, b_ada.reshape(depth, 1, n))
    return out[:, :b].reshape(depth, b, 6, d)


def _modulated_norm(x, g, shift, scale):
    ms = jnp.mean(x * x, axis=-1, keepdims=True)
    return (x * lax.rsqrt(ms + EPS) * g) * (1.0 + scale) + shift


def _inproj_kernel(x_ref, mod_ref, g1_ref, w_hbm, gq_ref, gk_ref, gmat_ref, o_ref, vt_ref,
                   w_ref, stage_ref, sem_ref, *, layer):
    groups = [slice(c * D_GROUP, (c + 1) * D_GROUP) for c in range(6)]

    def body(stager):
        if stager:
            stager.start()
        h = _modulated_norm(x_ref[...], g1_ref[...], mod_ref[0:1, :], mod_ref[1:2, :]).astype(BF16)
        for c, cols in enumerate(groups):
            if stager:
                stager.take(c)
            y = jnp.dot(h, w_ref[:, cols], preferred_element_type=F32)
            if c == 0:
                y = y * QK_SCALE
            elif c in (3, 4):
                msq = jnp.dot((y * y).astype(BF16), gmat_ref[...], preferred_element_type=F32)
                y = y * lax.rsqrt(msq + EPS) * (gq_ref[...] if c == 3 else gk_ref[...])
            o_ref[:, cols] = y.astype(BF16)
            if c in (2, 5):
                for p in range(PAIRS):
                    for blk in range(vt_ref.shape[1]):
                        vt_ref[(c // 3) * PAIRS + p, blk] = y[blk * SB_TQ:(blk + 1) * SB_TQ,
                                                              p * LANES:(p + 1) * LANES].T.astype(BF16)

    w_l = w_hbm.at[layer]
    stager = _WeightStager([(w_l.at[:, cols], _ref_setter(w_ref, (slice(None), cols))) for cols in groups],
                           stage_ref, sem_ref)
    pl.when(_first_step())(lambda: body(stager))
    pl.when(jnp.logical_not(_first_step()))(lambda: body(None))


def _inproj(x, mod, layer, g1, w_in, gq_t, gk_t, gmat):
    b, s, d = x.shape
    n = w_in.shape[2]
    tm = min(INPROJ_ROW_TILE, s)
    return pl.pallas_call(
        functools.partial(_inproj_kernel, layer=layer),
        grid=(b, s // tm),
        in_specs=[pl.BlockSpec((None, tm, d), lambda i, j: (i, j, 0)),
                  pl.BlockSpec((None, None, 6, d), lambda i, j: (layer, i, 0, 0)),
                  _resident((1, d), lambda i, j: (0, 0)),
                  pl.BlockSpec(memory_space=pl.ANY),
                  _resident((1, D_GROUP), lambda i, j: (0, 0)),
                  _resident((1, D_GROUP), lambda i, j: (0, 0)),
                  _resident((D_GROUP, D_GROUP), lambda i, j: (0, 0))],
        out_specs=[pl.BlockSpec((None, tm, n), lambda i, j: (i, j, 0)),
                   pl.BlockSpec((None, 2 * PAIRS, tm // SB_TQ, LANES, SB_TQ), lambda i, j: (i, 0, j, 0, 0))],
        out_shape=[jax.ShapeDtypeStruct((b, s, n), BF16),
                   jax.ShapeDtypeStruct((b, 2 * PAIRS, s // SB_TQ, LANES, SB_TQ), BF16)],
        scratch_shapes=[pltpu.VMEM((d, n), BF16), *_stage_scratch(d, D_GROUP)],
        compiler_params=_cparams(2),
        name="norm1_inproj",
    )(x, mod, g1, w_in, gq_t, gk_t, gmat)


def _head_masked(q):
    lane = lax.broadcasted_iota(jnp.int32, q.shape, 1)
    zero = jnp.zeros_like(q)
    return jnp.where(lane < HEAD_DIM, q, zero), jnp.where(lane >= HEAD_DIM, q, zero)


def _scores_t(k_blk, q_masked):
    return lax.dot_general(k_blk, q_masked, (((1,), (1,)), ((), ())), preferred_element_type=F32)


def _fill_vt(v_ref, vt_ref, blk):
    for i in range(vt_ref.shape[0]):
        vt_ref[i] = v_ref[i * blk:(i + 1) * blk, :].astype(F32).T.astype(BF16)


def _merge_heads_t(out_a, out_b):
    row = lax.broadcasted_iota(jnp.int32, out_a.shape, 0)
    return jnp.where(row < HEAD_DIM, out_a, out_b).T


def _split_bf16(x):
    hi = lax.bitcast_convert_type(lax.bitcast_convert_type(x, jnp.uint32) & jnp.uint32(0xFFFF0000), F32)
    return hi.astype(BF16), (x - hi).astype(BF16)


def _sb_kernel(q_ref, k_ref, vt_ref, o_ref, acc_ref):
    tq = vt_ref.shape[2]
    tk = tq
    n_tiles = q_ref.shape[0] // tq
    qi = pl.program_id(2)

    q_heads = [_head_masked(q_ref[t * tq:(t + 1) * tq, :]) for t in range(n_tiles)]
    row = lax.broadcasted_iota(jnp.int32, (tk, tq), 0)
    lane = lax.broadcasted_iota(jnp.int32, (tk, tq), 1)
    tri = (lane >= row).astype(BF16)
    strict = row < lane

    def blocks(work, carries, first):
        depth = max(len(kbs) for kbs in work.values())
        chains = [(t, n, h) for n in range(depth) for t in work if n < len(work[t]) for h in range(2)]
        k_blk = lambda kb: k_ref[pl.ds(pl.multiple_of(kb * tk, tk), tk), :]
        z = {(t, n, h): _scores_t(k_blk(work[t][n]), q_heads[t][h]) for t, n, h in chains}
        incl = {}
        carries = {t: list(carries[t]) for t in work}
        pv = {}

        def suffix_stage(c):
            if first and c[1] == 0:
                z[c] = jnp.where(strict, z[c], NEG_INF)
            sp = jnp.maximum(z[c], jnp.log(1.0 + jnp.exp2(jnp.minimum(z[c], SOFTPLUS_LINEAR) * LOG2E)))
            hi, lo = _split_bf16(sp)
            incl[c] = (jnp.dot(tri, hi, preferred_element_type=F32) +
                       jnp.dot(tri, lo, preferred_element_type=F32))

        def weight_stage(c):
            t, n, h = c
            log_w = z[c] - incl[c]
            if not (first and n == 0):
                log_w = log_w + carries[t][h]
            out = jnp.dot(vt_ref[work[t][n]], jnp.exp(log_w).astype(BF16), preferred_element_type=F32)
            pv[t, h] = out if (t, h) not in pv else pv[t, h] + out
            carries[t][h] = carries[t][h] - incl[c][0:1, :]

        for c in chains:
            suffix_stage(c)
        for c in chains:
            weight_stage(c)
        for t, h in pv:
            if first:
                acc_ref[t, h] = pv[t, h]
            else:
                acc_ref[t, h] += pv[t, h]
        return tuple(tuple(carries[t]) for t in work)

    zero = jnp.zeros((1, tq), F32)
    zeros = {t: (zero, zero) for t in range(n_tiles)}
    base = qi * n_tiles
    full = {t: [base + t, base + t - 1] for t in range(n_tiles)}
    head = dict(full)
    head[0] = [base]
    carries = lax.cond(qi == 0,
                       lambda: blocks(head, zeros, first=True),
                       lambda: blocks(full, zeros, first=True))

    def cond(state):
        kb, ca, cb = state
        alive = jnp.max(jnp.maximum(ca, cb)) > SB_DEAD
        return jnp.logical_and(kb >= 0, alive)

    least_dead = carries[0][0]
    for t in range(n_tiles):
        for h in range(2):
            least_dead = jnp.maximum(least_dead, carries[t][h])

    @pl.when(jnp.max(least_dead) > SB_DEAD)
    def _():
        for t in range(n_tiles):
            def body(state, t=t):
                kb, ca, cb = state
                ((ca, cb),) = blocks({t: [kb]}, {t: (ca, cb)}, first=False)
                return kb - 1, ca, cb

            lax.while_loop(cond, body, (base + t - 2,) + carries[t])

    for t in range(n_tiles):
        o_ref[t * tq:(t + 1) * tq, :] = _merge_heads_t(acc_ref[t, 0], acc_ref[t, 1]).astype(o_ref.dtype)


def _sb_attention(proj, vt):
    b, s, _ = proj.shape
    tq = SB_TQ
    step = min(SB_TILES * tq, s)
    return pl.pallas_call(
        _sb_kernel,
        grid=(b, PAIRS, s // step),
        in_specs=[pl.BlockSpec((None, step, LANES), lambda i, p, j: (i, j, p)),
                  pl.BlockSpec((None, s, LANES), lambda i, p, j: (i, 0, PAIRS + p)),
                  pl.BlockSpec((None, None, s // tq, LANES, tq), lambda i, p, j: (i, p, 0, 0, 0))],
        out_specs=pl.BlockSpec((None, step, LANES), lambda i, p, j: (i, j, p)),
        out_shape=jax.ShapeDtypeStruct((b, s, D_GROUP), BF16),
        scratch_shapes=[pltpu.VMEM((step // tq, 2, LANES, tq), F32)],
        compiler_params=_cparams(3),
        name="stickbreak_attn",
    )(proj, proj, vt)


def _ca_band_rows(i, half, tq):
    start = half * LANES
    stop = (half + 1) * LANES - CHUNK + BAND
    lo = min(max(start, i * tq), (i + 1) * tq) - i * tq
    hi = max(min(stop, (i + 1) * tq), i * tq) - i * tq
    return lo, max(hi, lo)


def _ca_kernel(q_ref, k_ref, vt_in_ref, bias_ref, o_ref, vt_ref, s_ref):
    tq = vt_ref.shape[3]
    n_tiles = q_ref.shape[0] // tq
    n_win = CA_WIN // tq
    qi = pl.program_id(2)

    @pl.when(qi == 0)
    def _():
        row = lax.broadcasted_iota(jnp.int32, (LANES, tq), 0)
        for i in range(vt_ref.shape[1]):
            vt = vt_in_ref[i]
            ones = jnp.ones_like(vt)
            vt_ref[0, i] = jnp.where(row < HEAD_DIM, vt, ones)
            vt_ref[1, i] = jnp.where(row >= HEAD_DIM, vt, ones)

    units = [(t, h) for t in range(n_tiles) for h in range(2)]
    halves = [slice(f * LANES, (f + 1) * LANES) for f in range(tq // LANES)]
    q_heads = [_head_masked(q_ref[t * tq:(t + 1) * tq, :]) for t in range(n_tiles)]
    first_blk = [qi * n_tiles + t - (n_win - 1) for t in range(n_tiles)]
    blks = [[jnp.maximum(first_blk[t] + i, 0) for i in range(n_win)] for t in range(n_tiles)]
    k_blk = lambda bi: k_ref[pl.ds(pl.multiple_of(bi * tq, tq), tq), :]

    def attend(sequence_start):
        col_max = {}
        for t, h in units:
            for i in range(n_win):
                raw = _scores_t(k_blk(blks[t][i]), q_heads[t][h])
                for f, lanes in enumerate(halves):
                    lo, hi = _ca_band_rows(i, f, tq)
                    if hi == lo:
                        continue
                    rows = slice(i * tq + lo, i * tq + hi)
                    sc = raw[lo:hi, lanes] + bias_ref[h, rows, lanes]
                    if sequence_start and t + i < n_win - 1:
                        sc = jnp.where(first_blk[t] + i >= 0, sc, NEG_INF)
                    s_ref[t, h, rows, lanes] = sc
                    blk_max = sc.max(axis=0, keepdims=True)
                    col_max[t, h, f] = (jnp.maximum(col_max[t, h, f], blk_max)
                                        if (t, h, f) in col_max else blk_max)
        for t in range(n_tiles):
            outs = []
            for h in range(2):
                acc = None
                for i in range(n_win):
                    cols = []
                    for f, lanes in enumerate(halves):
                        lo, hi = _ca_band_rows(i, f, tq)
                        parts = [jnp.zeros((lo, LANES), BF16)] if lo else []
                        if hi > lo:
                            rows = slice(i * tq + lo, i * tq + hi)
                            parts.append(jnp.exp(s_ref[t, h, rows, lanes] - col_max[t, h, f]).astype(BF16))
                        if hi < tq:
                            parts.append(jnp.zeros((tq - hi, LANES), BF16))
                        cols.append(jnp.concatenate(parts, axis=0) if len(parts) > 1 else parts[0])
                    p = jnp.concatenate(cols, axis=1)
                    out = jnp.dot(vt_ref[h, blks[t][i]], p, preferred_element_type=F32)
                    acc = out if acc is None else acc + out
                ones_row = HEAD_DIM if h == 0 else 0
                outs.append(acc * (1.0 / acc[ones_row:ones_row + 1, :]))
            o_ref[t * tq:(t + 1) * tq, :] = _merge_heads_t(outs[0], outs[1]).astype(o_ref.dtype)

    n_start_steps = -(-(n_win - 1) // n_tiles)
    pl.when(qi < n_start_steps)(lambda: attend(True))
    pl.when(qi >= n_start_steps)(lambda: attend(False))


def _ca_attention(proj, vt, bias_tab, layer):
    b, s, _ = proj.shape
    tq = CA_TQ
    step = min(CA_TILES * tq, s)
    base = 3 * PAIRS
    return pl.pallas_call(
        _ca_kernel,
        grid=(b, PAIRS, s // step),
        in_specs=[pl.BlockSpec((None, step, LANES), lambda i, p, j: (i, j, base + p)),
                  pl.BlockSpec((None, s, LANES), lambda i, p, j: (i, 0, base + PAIRS + p)),
                  pl.BlockSpec((None, None, s // tq, LANES, tq), lambda i, p, j: (i, PAIRS + p, 0, 0, 0)),
                  pl.BlockSpec((2, CA_WIN, tq), lambda i, p, j: (layer * PAIRS + p, 0, 0))],
        out_specs=pl.BlockSpec((None, step, LANES), lambda i, p, j: (i, j, p)),
        out_shape=jax.ShapeDtypeStruct((b, s, D_GROUP), BF16),
        scratch_shapes=[pltpu.VMEM((2, s // tq, LANES, tq), BF16),
                        pltpu.VMEM((step // tq, 2, CA_WIN, tq), F32)],
        compiler_params=_cparams(3),
        name="chunkrel_attn",
    )(proj, proj, vt, bias_tab)


BIAS_ROWS = 128
BIAS_EXT = CA_WIN + CA_TQ


def _ca_bias_kernel(ext_ref, o_ref):
    width = CA_TQ + BIAS_ROWS
    r_loc = lax.broadcasted_iota(jnp.int32, (BIAS_ROWS, CA_TQ), 0)
    col = lax.broadcasted_iota(jnp.int32, (BIAS_ROWS, CA_TQ), 1)
    chunk_start = (col // CHUNK) * CHUNK
    for a in range(CA_WIN // BIAS_ROWS):
        k_in_band = r_loc + (a * BIAS_ROWS) - chunk_start
        in_band = jnp.logical_and(k_in_band >= 0, k_in_band < BAND)
        start = CA_WIN - BIAS_ROWS * (a + 1)
        for h in range(o_ref.shape[0]):
            m = jnp.broadcast_to(ext_ref[h, :, start:start + width], (BIAS_ROWS, width))
            m = pltpu.roll(m, 0, 1, stride=1, stride_axis=0)
            o_ref[h, a * BIAS_ROWS:(a + 1) * BIAS_ROWS, :] = jnp.where(
                in_band, m[:, BIAS_ROWS:BIAS_ROWS + CA_TQ], NEG_INF)


def _ca_bias_table(rel_bias):
    depth, n_heads, n_rel = rel_bias.shape
    rb = rel_bias.reshape(depth * n_heads, n_rel).astype(F32)
    n_lo = CA_TQ - REL_CLIP
    n_hi = BIAS_EXT - n_lo - n_rel
    ext = jnp.concatenate([jnp.broadcast_to(rb[:, :1], (rb.shape[0], n_lo)), rb,
                           jnp.broadcast_to(rb[:, -1:], (rb.shape[0], n_hi))], axis=1)
    return pl.pallas_call(
        _ca_bias_kernel,
        grid=(depth,),
        in_specs=[pl.BlockSpec((n_heads, 1, BIAS_EXT), lambda i: (i, 0, 0))],
        out_specs=pl.BlockSpec((n_heads, CA_WIN, CA_TQ), lambda i: (i, 0, 0)),
        out_shape=jax.ShapeDtypeStruct((depth * n_heads, CA_WIN, CA_TQ), F32),
        compiler_params=_cparams(1),
        name="ca_bias_table",
    )(ext.reshape(depth * n_heads, 1, BIAS_EXT))


def _mlp_kernel(x_ref, osb_ref, oca_ref, mod_ref, g2_ref, wo_hbm, w1_hbm, w2_hbm, o_ref,
                wo_ref, w1_ref, w2_ref, stage_ref, sem_ref, *, layer):
    d, d_ff = w1_ref.shape
    fc = stage_ref.shape[2]
    chunks = [slice(c * fc, (c + 1) * fc) for c in range(d_ff // fc)]

    def body(stager):
        if stager:
            stager.start()
            stager.take(0)
        att = (jnp.dot(osb_ref[...], wo_ref[0:D_GROUP, :], preferred_element_type=F32) +
               jnp.dot(oca_ref[...], wo_ref[D_GROUP:, :], preferred_element_type=F32))
        x1 = x_ref[...] + mod_ref[2:3, :] * att
        h = _modulated_norm(x1, g2_ref[...], mod_ref[3:4, :], mod_ref[4:5, :]).astype(BF16)
        acc = jnp.zeros(x1.shape, F32)
        for c, cols in enumerate(chunks):
            if stager:
                stager.take(1 + 2 * c)
            u = jnp.maximum(jnp.dot(h, w1_ref[:, cols], preferred_element_type=F32), 0.0)
            if stager:
                stager.take(2 + 2 * c)
            acc = acc + jnp.dot((u * u).astype(BF16), w2_ref[cols, :], preferred_element_type=F32)
        o_ref[...] = x1 + mod_ref[5:6, :] * acc

    wo_l, w1_l, w2_l = wo_hbm.at[layer], w1_hbm.at[layer], w2_hbm.at[layer]
    blocks = [(wo_l, _ref_setter(wo_ref, (slice(None), slice(None))))]
    for cols in chunks:
        blocks.append((w1_l.at[:, cols], _ref_setter(w1_ref, (slice(None), cols))))
        blocks.append((w2_l.at[cols, :], _ref_setter(w2_ref, (cols, slice(None)))))
    stager = _WeightStager(blocks, stage_ref, sem_ref)
    pl.when(_first_step())(lambda: body(stager))
    pl.when(jnp.logical_not(_first_step()))(lambda: body(None))


def _outproj_mlp(x, o_sb, o_ca, mod, layer, g2, w_o, w1, w2):
    b, s, d = x.shape
    d_ff = w1.shape[2]
    tm = min(ROW_TILE, s)
    hbm = pl.BlockSpec(memory_space=pl.ANY)
    return pl.pallas_call(
        functools.partial(_mlp_kernel, layer=layer),
        grid=(b, s // tm),
        in_specs=[pl.BlockSpec((None, tm, d), lambda i, j: (i, j, 0)),
                  pl.BlockSpec((None, tm, D_GROUP), lambda i, j: (i, j, 0)),
                  pl.BlockSpec((None, tm, D_GROUP), lambda i, j: (i, j, 0)),
                  pl.BlockSpec((None, None, 6, d), lambda i, j: (layer, i, 0, 0)),
                  _resident((1, d), lambda i, j: (0, 0)),
                  hbm, hbm, hbm],
        out_specs=pl.BlockSpec((None, tm, d), lambda i, j: (i, j, 0)),
        out_shape=jax.ShapeDtypeStruct((b, s, d), F32),
        scratch_shapes=[pltpu.VMEM((d, d), BF16), pltpu.VMEM((d, d_ff), BF16), pltpu.VMEM((d_ff, d), BF16),
                        *_stage_scratch(d, d)],
        compiler_params=_cparams(2),
        name="outproj_mlp",
    )(x, o_sb, o_ca, mod, g2, w_o, w1, w2)


def kernel(x, c, g_norm1, w_in, g_q, g_k, rel_bias, w_o, g_norm2, w1, w2, w_ada, b_ada):
    depth = w_in.shape[0]
    mod = _ada_modulation(c, w_ada, b_ada)
    bias_tab = _ca_bias_table(rel_bias)
    lane_head = jnp.arange(D_GROUP) // HEAD_DIM
    gmat = ((lane_head[:, None] == lane_head[None, :]).astype(F32) * (1.0 / HEAD_DIM)).astype(BF16)
    for l in range(depth):
        gq_t = (jnp.tile(g_q[l], HEADS_PER_GROUP) * QK_SCALE)[None, :]
        gk_t = jnp.tile(g_k[l], HEADS_PER_GROUP)[None, :]
        proj, vt = _inproj(x, mod, l, g_norm1[l][None, :], w_in, gq_t, gk_t, gmat)
        o_sb = _sb_attention(proj, vt)
        o_ca = _ca_attention(proj, vt, bias_tab, l)
        x = _outproj_mlp(x, o_sb, o_ca, mod, l, g_norm2[l][None, :], w_o, w1, w2)
    return x
```
